```python
import math
import jax
import jax.numpy as jnp
from jax import lax
import numpy as np


D_MODEL = 1024
BATCH = 8
SEQ = 4096
DEPTH = 2

GRID_W = 64
CTX_LEN = 256
N_BRANCH = 3
MIX_W = 512
SHORT_CONV = 4
DN_H = 4
DN_DK = 128
DN_DV = MIX_W // DN_H
DN_W = DN_H * DN_DK
DN_CHUNK = 64
LRU_W = MIX_W
LRU_G = 8
LRU_BS = LRU_W // LRU_G
LRU_C = 8.0
DA_H = 4
DA_D = 64
DA_DV = MIX_W // DA_H
DA_QW = DA_H * 2 * DA_D
ATTN_BLOCK = 128
ROPE_BASE = 10000.0
ROPE_NF = DA_D // 4
D_FF = -(-8 * D_MODEL // (3 * 256)) * 256
SPLIT = (DN_W, DN_W, DN_H * DN_DV, DN_H * DN_DV, 2 * DN_H, 2 * DN_H, LRU_W, LRU_W, DA_QW, DA_QW, DA_H * DA_DV, N_BRANCH * D_MODEL)
IN_COLS = sum(SPLIT)

kernel_name = 'hybrid_gdn_rglru_diffattn_prefix_dit'


def rmsnorm(x, w, eps=1e-6):
    xf = x.astype(jnp.float32)
    y = xf * lax.rsqrt(jnp.mean(xf * xf, axis=-1, keepdims=True) + eps) * w.astype(jnp.float32)
    return y.astype(x.dtype)


def l2norm(x, eps=1e-6):
    xf = x.astype(jnp.float32)
    return xf * lax.rsqrt(jnp.sum(xf * xf, axis=-1, keepdims=True) + eps)


def modulate(h, shift, scale):
    return h * (1.0 + scale[:, None]) + shift[:, None]


def split_cols(p):
    out, start = [], 0
    for size in SPLIT:
        out.append(p[..., start:start + size])
        start += size
    return out


def flip_t(t):
    return jnp.flip(t, axis=1)


def ident_t(t):
    return t


def centred_dwconv(x, w):
    K = w.shape[0]
    left = (K - 1) // 2
    T = x.shape[1]
    xp = jnp.pad(x, ((0, 0), (left, K - 1 - left), (0, 0)))
    y = xp[:, 0:T] * w[0]
    for j in range(1, K):
        y = y + xp[:, j:j + T] * w[j]
    return y


def axial_angles(n_tokens):
    rows = n_tokens // GRID_W
    row = jnp.repeat(jnp.arange(rows, dtype=jnp.float32), GRID_W)
    col = jnp.tile(jnp.arange(GRID_W, dtype=jnp.float32), rows)
    inv = ROPE_BASE ** (-jnp.arange(ROPE_NF, dtype=jnp.float32) / ROPE_NF)
    return jnp.stack([row[:, None] * inv, col[:, None] * inv], axis=1)


def rope_2d(x, rope_cs):
    cos, sin = rope_cs
    B, T, H, Dh = x.shape
    xr = x.astype(jnp.float32).reshape(B, T, H, 2, 2, Dh // 4)
    x1, x2 = xr[..., 0, :], xr[..., 1, :]
    cos = cos[None, :, None]
    sin = sin[None, :, None]
    out = jnp.stack([x1 * cos - x2 * sin, x2 * cos + x1 * sin], axis=-2)
    return out.reshape(B, T, H, Dh).astype(x.dtype)


def gated_delta_chunked(q, k, v, g, beta, s0):
    B, T, H, DK = q.shape
    DV = v.shape[-1]
    C = DN_CHUNK
    N = T // C

    def to_chunks(t):
        t = t.astype(jnp.float32).reshape((B, N, C, H) + t.shape[3:])
        return jnp.moveaxis(t, (1, 3), (0, 2))

    q = to_chunks(q) * (DK ** -0.5)
    k = to_chunks(k)
    v = to_chunks(v)
    beta = to_chunks(beta)
    g = jnp.cumsum(to_chunks(g), axis=-1)
    incl = jnp.tril(jnp.ones((C, C), dtype=bool))
    strict = jnp.tril(jnp.ones((C, C), dtype=bool), -1)
    diff = g[..., :, None] - g[..., None, :]
    decay = jnp.where(incl, jnp.exp(jnp.where(incl, diff, 0.0)), 0.0)
    kb = k * beta[..., None]
    lower = jnp.where(strict, jnp.einsum('nbhid,nbhjd->nbhij', kb, k) * decay, 0.0)
    a_mat = lower + jnp.eye(C, dtype=jnp.float32)
    w = lax.linalg.triangular_solve(a_mat, kb * jnp.exp(g)[..., None], left_side=True, lower=True, unit_diagonal=True)
    u = lax.linalg.triangular_solve(a_mat, v * beta[..., None], left_side=True, lower=True, unit_diagonal=True)
    qk = jnp.where(incl, jnp.einsum('nbhid,nbhjd->nbhij', q, k) * decay, 0.0)
    q_in = q * jnp.exp(g)[..., None]
    k_out = k * jnp.exp(g[..., -1:] - g)[..., None]
    g_end = jnp.exp(g[..., -1])

    def step(s, inp):
        q_n, w_n, u_n, qk_n, ko_n, ge_n = inp
        v_new = u_n - jnp.einsum('bhck,bhkv->bhcv', w_n, s)
        o = jnp.einsum('bhck,bhkv->bhcv', q_n, s) + jnp.einsum('bhij,bhjv->bhiv', qk_n, v_new)
        s = s * ge_n[..., None, None] + jnp.einsum('bhck,bhcv->bhkv', ko_n, v_new)
        return s, o

    s_fin, o = lax.scan(step, s0.astype(jnp.float32), (q_in, w, u, qk, k_out, g_end))
    o = jnp.moveaxis(o, (0, 2), (1, 3)).reshape(B, T, H, DV)
    return o, s_fin


def deltanet_branch(pc, pl, conv_w, a_log, dt_bias, out_norm, need_ctx):
    def prep(q, k, v, a, b):
        B, T, _ = q.shape
        qkv = jax.nn.silu(centred_dwconv(jnp.concatenate([q, k, v], axis=-1), conv_w))
        q, k, v = jnp.split(qkv, [DN_W, 2 * DN_W], axis=-1)
        q = l2norm(q.reshape(B, T, DN_H, DN_DK))
        k = l2norm(k.reshape(B, T, DN_H, DN_DK))
        v = v.reshape(B, T, DN_H, DN_DV)
        g = -jnp.exp(a_log.astype(jnp.float32)) * jax.nn.softplus(a.reshape(B, T, 2, DN_H).astype(jnp.float32) + dt_bias.astype(jnp.float32))
        bt = jax.nn.sigmoid(b.reshape(B, T, 2, DN_H).astype(jnp.float32))
        return q, k, v, g, bt

    qc, kc, vc, gc, bc = prep(pc[0], pc[1], pc[2], pc[4], pc[5])
    ql, kl, vl, gl, bl = prep(pl[0], pl[1], pl[2], pl[4], pl[5])
    B = ql.shape[0]
    outs_c, outs_l = [], []
    for d in range(2):
        f = flip_t if d else ident_t
        s0 = jnp.zeros((B, DN_H, DN_DK, DN_DV), jnp.float32)
        oc, s_ctx = gated_delta_chunked(f(qc), f(kc), f(vc), f(gc[:, :, d]), f(bc[:, :, d]), s0)
        ol, _ = gated_delta_chunked(f(ql), f(kl), f(vl), f(gl[:, :, d]), f(bl[:, :, d]), s_ctx)
        outs_c.append(f(oc))
        outs_l.append(f(ol))

    def finish(o, z):
        B, T = z.shape[:2]
        o = rmsnorm(o, out_norm) * jax.nn.silu(z.reshape(B, T, DN_H, DN_DV).astype(jnp.float32))
        return o.reshape(B, T, DN_H * DN_DV).astype(z.dtype)

    y_l = finish(outs_l[0] + outs_l[1], pl[3])
    y_c = finish(outs_c[0] + outs_c[1], pc[3]) if need_ctx else None
    return y_c, y_l


def rglru_coeffs(xb, wa, ba, wi, bi, lam):
    B, T, W = xb.shape
    xg = xb.reshape(B, T, LRU_G, LRU_BS)
    r = jax.nn.sigmoid((jnp.einsum('btgi,gij->btgj', xg, wa).reshape(B, T, W) + ba).astype(jnp.float32))
    i = jax.nn.sigmoid((jnp.einsum('btgi,gij->btgj', xg, wi).reshape(B, T, W) + bi).astype(jnp.float32))
    log_a = -LRU_C * r * jax.nn.softplus(-lam.astype(jnp.float32))
    a = jnp.exp(log_a)
    b = jnp.sqrt(1.0 - jnp.exp(2.0 * log_a)) * (i * xb.astype(jnp.float32))
    return a, b


def linear_scan(a, b, h0):
    def combine(e1, e2):
        a1, b1 = e1
        a2, b2 = e2
        return a1 * a2, a2 * b1 + b2
    a_cum, b_cum = lax.associative_scan(combine, (a, b), axis=1)
    return b_cum + a_cum * h0[:, None, :]


def rglru_branch(pc, pl, conv_w, conv_b, wa, ba, wi, bi, lam, need_ctx):
    xc = centred_dwconv(pc[0], conv_w) + conv_b
    xl = centred_dwconv(pl[0], conv_w) + conv_b
    B = xl.shape[0]
    hs_c, hs_l = [], []
    for d in range(2):
        f = flip_t if d else ident_t
        a, b = rglru_coeffs(f(xc), wa[d], ba[d], wi[d], bi[d], lam[d])
        h_c = linear_scan(a, b, jnp.zeros((B, LRU_W), jnp.float32))
        a, b = rglru_coeffs(f(xl), wa[d], ba[d], wi[d], bi[d], lam[d])
        h_l = linear_scan(a, b, h_c[:, -1])
        hs_c.append(f(h_c))
        hs_l.append(f(h_l))

    def finish(h, y):
        return (h * jax.nn.gelu(y.astype(jnp.float32))).astype(y.dtype)

    y_l = finish(hs_l[0] + hs_l[1], pl[1])
    y_c = finish(hs_c[0] + hs_c[1], pc[1]) if need_ctx else None
    return y_c, y_l


def diff_attend(q1, q2, k1, k2, v, lam):
    scale = DA_D ** -0.5
    s1 = jnp.einsum('bqhd,bkhd->bhqk', q1.astype(jnp.float32), k1) * scale
    s2 = jnp.einsum('bqhd,bkhd->bhqk', q2.astype(jnp.float32), k2) * scale
    p = jax.nn.softmax(s1, axis=-1) - lam * jax.nn.softmax(s2, axis=-1)
    return jnp.einsum('bhqk,bkhv->bqhv', p, v)


def diffattn_branch(pc, pl, rope_cs, lam_vecs, sub_norm, lam_init, need_ctx):
    def heads(q, k, v):
        B, T, _ = q.shape
        q = q.reshape(B, T, DA_H, 2, DA_D)
        k = k.reshape(B, T, DA_H, 2, DA_D)
        return q[..., 0, :], q[..., 1, :], k[..., 0, :], k[..., 1, :], v.reshape(B, T, DA_H, DA_DV)

    q1c, q2c, k1c, k2c, vc = heads(pc[0], pc[1], pc[2])
    q1l, q2l, k1l, k2l, vl = heads(pl[0], pl[1], pl[2])
    q1l = rope_2d(q1l, rope_cs)
    q2l = rope_2d(q2l, rope_cs)
    k1l = rope_2d(k1l, rope_cs)
    k2l = rope_2d(k2l, rope_cs)
    lv = lam_vecs.astype(jnp.float32)
    lam = jnp.exp(jnp.sum(lv[0] * lv[1])) - jnp.exp(jnp.sum(lv[2] * lv[3])) + lam_init
    k1c32, k2c32, vc32 = k1c.astype(jnp.float32), k2c.astype(jnp.float32), vc.astype(jnp.float32)
    k1 = jnp.concatenate([k1c32, k1l.astype(jnp.float32)], axis=1)
    k2 = jnp.concatenate([k2c32, k2l.astype(jnp.float32)], axis=1)
    v = jnp.concatenate([vc32, vl.astype(jnp.float32)], axis=1)
    B, T = q1l.shape[:2]
    nb = T // ATTN_BLOCK

    def blocks(t):
        return jnp.moveaxis(t.reshape(B, nb, ATTN_BLOCK, DA_H, DA_D), 1, 0)

    o_l = lax.map(lambda qq: diff_attend(qq[0], qq[1], k1, k2, v, lam), (blocks(q1l), blocks(q2l)))
    o_l = jnp.moveaxis(o_l, 0, 1).reshape(B, T, DA_H, DA_DV)

    def finish(o, like):
        o = rmsnorm(o, sub_norm, 1e-5) * (1.0 - lam_init)
        return o.reshape(o.shape[0], o.shape[1], DA_H * DA_DV).astype(like.dtype)

    y_l = finish(o_l, pl[2])
    if need_ctx:
        y_c = finish(diff_attend(q1c, q2c, k1c32, k2c32, vc32, lam), pc[2])
    else:
        y_c = None
    return y_c, y_l


def merge_branches(ya, yb, yc, gate_logits, w_branch, w_out):
    B, T, _ = ya.shape
    up = jnp.einsum('btnm,nmd->btnd', jnp.stack([ya, yb, yc], axis=2), w_branch)
    gates = jax.nn.sigmoid(gate_logits.reshape(B, T, N_BRANCH, D_MODEL))
    return jnp.einsum('btnd,btnd->btd', gates, up) @ w_out


def mixer_sublayer(u_c, u_l, rope_cs, lam_init, need_ctx, w_in, dn_conv, dn_a_log, dn_dt_bias, dn_norm, lru_conv_w, lru_conv_b, lru_wa, lru_ba, lru_wi, lru_bi, lru_lambda, da_lambda, da_norm, w_branch, w_out):
    pc = split_cols(u_c @ w_in)
    pl = split_cols(u_l @ w_in)
    ya_c, ya_l = deltanet_branch(pc[0:6], pl[0:6], dn_conv, dn_a_log, dn_dt_bias, dn_norm, need_ctx)
    yb_c, yb_l = rglru_branch(pc[6:8], pl[6:8], lru_conv_w, lru_conv_b, lru_wa, lru_ba, lru_wi, lru_bi, lru_lambda, need_ctx)
    yc_c, yc_l = diffattn_branch(pc[8:11], pl[8:11], rope_cs, da_lambda, da_norm, lam_init, need_ctx)
    y_l = merge_branches(ya_l, yb_l, yc_l, pl[11], w_branch, w_out)
    y_c = merge_branches(ya_c, yb_c, yc_c, pc[11], w_branch, w_out) if need_ctx else None
    return y_c, y_l


def swiglu(h, wg, wu, wd):
    return (jax.nn.silu(h @ wg) * (h @ wu)) @ wd


def setup_inputs(seed: int = 0) -> dict:
    key = jax.random.key(seed)
    ks = jax.random.split(key, 28)
    L, D = DEPTH, D_MODEL
    f32 = jnp.float32

    def nrm(k, shape, scale):
        return jax.random.normal(k, shape, f32) * scale

    dt = jnp.exp(jax.random.uniform(ks[11], (L, 2, DN_H), f32, minval=math.log(1e-3), maxval=math.log(1e-1)))
    a_pow = jax.random.uniform(ks[19], (L, 2, LRU_W), f32, minval=0.9, maxval=0.999)
    a_base = a_pow ** (1.0 / LRU_C)
    return {
        'x': nrm(ks[0], (BATCH, SEQ, D), 1.0),
        'c': nrm(ks[1], (BATCH, D), 1.0),
        'ctx': nrm(ks[2], (BATCH, CTX_LEN, D), 1.0),
        'c_ctx': nrm(ks[3], (D,), 1.0),
        'w_mod': nrm(ks[4], (L, D, 6 * D), D ** -0.5),
        'b_mod': nrm(ks[5], (L, 6 * D), 0.02),
        'norm_mix': 1.0 + nrm(ks[6], (L, D), 0.02),
        'norm_ffn': 1.0 + nrm(ks[7], (L, D), 0.02),
        'w_in': nrm(ks[8], (L, D, IN_COLS), D ** -0.5),
        'dn_conv': nrm(ks[9], (L, SHORT_CONV, 3 * DN_W), SHORT_CONV ** -0.5),
        'dn_a_log': jnp.log(jax.random.uniform(ks[10], (L, 2, DN_H), f32, minval=1.0, maxval=16.0)),
        'dn_dt_bias': dt + jnp.log(-jnp.expm1(-dt)),
        'dn_norm': 1.0 + nrm(ks[12], (L, DN_DV), 0.02),
        'lru_conv_w': nrm(ks[13], (L, SHORT_CONV, LRU_W), SHORT_CONV ** -0.5),
        'lru_conv_b': nrm(ks[14], (L, LRU_W), 0.01),
        'lru_wa': nrm(ks[15], (L, 2, LRU_G, LRU_BS, LRU_BS), LRU_BS ** -0.5),
        'lru_ba': nrm(ks[16], (L, 2, LRU_W), 0.01),
        'lru_wi': nrm(ks[17], (L, 2, LRU_G, LRU_BS, LRU_BS), LRU_BS ** -0.5),
        'lru_bi': nrm(ks[18], (L, 2, LRU_W), 0.01),
        'lru_lambda': jnp.log(a_base) - jnp.log1p(-a_base),
        'da_lambda': nrm(ks[20], (L, 4, DA_D), 0.1),
        'da_norm': 1.0 + nrm(ks[21], (L, DA_DV), 0.02),
        'w_branch': nrm(ks[22], (L, N_BRANCH, MIX_W, D), MIX_W ** -0.5),
        'w_out': nrm(ks[23], (L, D, D), D ** -0.5),
        'w_ffn_gate': nrm(ks[24], (L, D, D_FF), D ** -0.5),
        'w_ffn_up': nrm(ks[25], (L, D, D_FF), D ** -0.5),
        'w_ffn_down': nrm(ks[26], (L, D_FF, D), D_FF ** -0.5),
        'norm_final': 1.0 + nrm(ks[27], (D,), 0.02),
    }


def reference(x, c, ctx, c_ctx, w_mod, b_mod, norm_mix, norm_ffn, w_in, dn_conv, dn_a_log, dn_dt_bias, dn_norm, lru_conv_w, lru_conv_b, lru_wa, lru_ba, lru_wi, lru_bi, lru_lambda, da_lambda, da_norm, w_branch, w_out, w_ffn_gate, w_ffn_up, w_ffn_down, norm_final):
    ang = axial_angles(x.shape[1])
    rope_cs = (jnp.cos(ang), jnp.sin(ang))
    s_lat = jax.nn.silu(c)
    s_ctx = jax.nn.silu(c_ctx)[None]
    h_lat, h_ctx = x, ctx
    for l in range(DEPTH):
        need_ctx = l < DEPTH - 1
        lam_init = 0.8 - 0.6 * math.exp(-0.3 * l)
        mod_l = jnp.split(s_lat @ w_mod[l] + b_mod[l], 6, axis=-1)
        mod_c = jnp.split(s_ctx @ w_mod[l] + b_mod[l], 6, axis=-1)
        u_l = modulate(rmsnorm(h_lat, norm_mix[l]), mod_l[0], mod_l[1])
        u_c = modulate(rmsnorm(h_ctx, norm_mix[l]), mod_c[0], mod_c[1])
        y_c, y_l = mixer_sublayer(u_c, u_l, rope_cs, lam_init, need_ctx, w_in[l], dn_conv[l], dn_a_log[l], dn_dt_bias[l], dn_norm[l], lru_conv_w[l], lru_conv_b[l], lru_wa[l], lru_ba[l], lru_wi[l], lru_bi[l], lru_lambda[l], da_lambda[l], da_norm[l], w_branch[l], w_out[l])
        h_lat = h_lat + mod_l[2][:, None] * y_l
        h_lat = h_lat + mod_l[5][:, None] * swiglu(modulate(rmsnorm(h_lat, norm_ffn[l]), mod_l[3], mod_l[4]), w_ffn_gate[l], w_ffn_up[l], w_ffn_down[l])
        if need_ctx:
            h_ctx = h_ctx + mod_c[2][:, None] * y_c
            h_ctx = h_ctx + mod_c[5][:, None] * swiglu(modulate(rmsnorm(h_ctx, norm_ffn[l]), mod_c[3], mod_c[4]), w_ffn_gate[l], w_ffn_up[l], w_ffn_down[l])
    return rmsnorm(h_lat, norm_final)
```

```python
import functools
import math

import jax
import jax.numpy as jnp
from jax import lax
from jax.experimental import pallas as pl
from jax.experimental.pallas import tpu as pltpu

F32 = jnp.float32
BF16 = jnp.bfloat16

D_MODEL = 1024
CTX_LEN = 256
GRID_W = 64
MIX_W = 512
N_HEAD = 4
HEAD_W = 128
DN_CHUNK = 64
LRU_C = 8.0
LRU_G = 8
ROPE_BASE = 10000.0
ROPE_NF = 16
D_FF = 2816
TM = 256
FF_CHUNK = 256
VMEM_LIMIT = 56 * 1024 * 1024

C_DNQKV, C_DNZ, C_LRUX, C_LRUY, C_DAQKV, C_GATE, C_AB = 0, 1536, 2048, 2560, 3072, 4608, 7680
IN_COLS_PAD = 7808


def _cparams(sem):
    return pltpu.CompilerParams(dimension_semantics=sem, vmem_limit_bytes=VMEM_LIMIT)


def _sigmoid(x):
    return 1.0 / (1.0 + jnp.exp(-x))


def _softplus(x):
    return jnp.maximum(x, 0.0) + jnp.log(1.0 + jnp.exp(-jnp.abs(x)))


def _dot(a, b):
    return jnp.dot(a, b, preferred_element_type=F32)


def _dot_nt(a, b):
    return lax.dot_general(a, b, (((1,), (1,)), ((), ())), preferred_element_type=F32)


def _dot_tn(a, b):
    return lax.dot_general(a, b, (((0,), (0,)), ((), ())), preferred_element_type=F32)


def _mod_kernel(s_ref, w_ref, b_ref, o_ref):
    s = s_ref[...]
    s = s * _sigmoid(s)
    o_ref[...] = jnp.dot(s, w_ref[...], preferred_element_type=F32,
                         precision=lax.Precision.HIGHEST) + b_ref[...]


def _mod_call(rows, w_mod, b_mod):
    depth = w_mod.shape[0]
    n6 = w_mod.shape[2]
    tn = 1024
    return pl.pallas_call(
        _mod_kernel,
        grid=(depth, n6 // tn),
        in_specs=[
            pl.BlockSpec((16, D_MODEL), lambda l, j: (0, 0)),
            pl.BlockSpec((None, D_MODEL, tn), lambda l, j: (l, 0, j)),
            pl.BlockSpec((None, 1, tn), lambda l, j: (l, 0, j)),
        ],
        out_specs=pl.BlockSpec((None, 16, tn), lambda l, j: (l, 0, j)),
        out_shape=jax.ShapeDtypeStruct((depth, 16, n6), F32),
        compiler_params=_cparams(("arbitrary", "arbitrary")),
        name="adaln_mod",
    )(rows, w_mod, b_mod.reshape(depth, 1, n6))


def _mod_index(b, i):
    return (b, jnp.minimum(i, 1), 0, 0)


def _norm_modulate(x, nw, shift, scale, eps=1e-6):
    ms = jnp.mean(x * x, axis=-1, keepdims=True)
    y = x * lax.rsqrt(ms + eps) * nw
    return y * (1.0 + scale) + shift


_INPROJ_SEGS = (
    (C_DNQKV, 1536), (C_DNZ, 512), (C_LRUX, 512), (C_LRUY, 512), (C_DAQKV, 1536), (C_GATE, 3072), (C_AB, 128))


def _inproj_kernel(h_ref, mod_ref, nw_ref, w_ref, *out_refs):
    u = _norm_modulate(h_ref[...], nw_ref[...], mod_ref[0:1, :], mod_ref[1:2, :]).astype(BF16)
    for (c0, width), o_ref in zip(_INPROJ_SEGS, out_refs):
        step = min(width, 512)
        for c in range(0, width, step):
            o_ref[:, c:c + step] = _dot(u, w_ref[:, c0 + c:c0 + c + step]).astype(o_ref.dtype)


def _inproj_call(h, mod, nw, w):
    B, T, _ = h.shape
    nt = T // TM
    dts = (BF16, BF16, F32, BF16, BF16, BF16, F32)
    return pl.pallas_call(
        _inproj_kernel,
        grid=(B, nt),
        in_specs=[
            pl.BlockSpec((None, TM, D_MODEL), lambda b, i: (b, i, 0)),
            pl.BlockSpec((None, None, 6, D_MODEL), _mod_index),
            pl.BlockSpec((1, D_MODEL), lambda b, i: (0, 0)),
            pl.BlockSpec((D_MODEL, IN_COLS_PAD), lambda b, i: (0, 0)),
        ],
        out_specs=[pl.BlockSpec((None, TM, wd), lambda b, i: (b, i, 0)) for _, wd in _INPROJ_SEGS],
        out_shape=[jax.ShapeDtypeStruct((B, T, wd), dt) for (_, wd), dt in zip(_INPROJ_SEGS, dts)],
        compiler_params=_cparams(("arbitrary", "arbitrary")),
        name="in_proj",
    )(h, mod, nw, w)


def _dwconv_tile(x, prev_last, next0, next1, w, row):
    R = x.shape[0]
    xm1 = jnp.where(row == 0, prev_last, pltpu.roll(x, 1, 0))
    xp1 = jnp.where(row == R - 1, next0, pltpu.roll(x, R - 1, 0))
    xp2 = jnp.where(row == R - 2, next0, jnp.where(row == R - 1, next1, pltpu.roll(x, R - 2, 0)))
    return w[0:1, :] * xm1 + w[1:2, :] * x + w[2:3, :] * xp1 + w[3:4, :] * xp2


def _halo_valid(i, nt):
    vp = (i >= 2).astype(F32)
    vn = jnp.logical_and(i >= 1, i <= nt - 2).astype(F32)
    return vp, vn


def _dn_prep_kernel(x_ref, xp_ref, xn_ref, ab_ref, cw_ref, al_ref, dtb_ref, o_ref, gb_ref, *, nt):
    i = pl.program_id(1)
    j = pl.program_id(2)
    vp, vn = _halo_valid(i, nt)
    x = x_ref[...].astype(F32)
    xp = xp_ref[...].astype(F32)
    xn = xn_ref[...].astype(F32)
    row = lax.broadcasted_iota(jnp.int32, x.shape, 0)
    y = _dwconv_tile(x, xp[15:16, :] * vp, xn[0:1, :] * vn, xn[1:2, :] * vn, cw_ref[...], row)
    s = y * _sigmoid(y)
    is_v = (j == 2).astype(F32)
    qscale = jnp.where(j == 0, HEAD_W ** -0.5, 1.0).astype(F32)
    for h in range(N_HEAD):
        sh = s[:, h * HEAD_W:(h + 1) * HEAD_W]
        ss = jnp.sum(sh * sh, axis=-1, keepdims=True)
        nrm = lax.rsqrt(ss + 1e-6) * qscale
        o_ref[:, h * HEAD_W:(h + 1) * HEAD_W] = (sh * (is_v + (1.0 - is_v) * nrm)).astype(o_ref.dtype)

    @pl.when(j == 0)
    def _():
        ab = ab_ref[...]
        lane = lax.broadcasted_iota(jnp.int32, ab.shape, 1)
        g = -jnp.exp(al_ref[...]) * _softplus(ab + dtb_ref[...])
        bt = _sigmoid(ab)
        r = lax.broadcasted_iota(jnp.int32, (TM, TM), 0)
        c = lax.broadcasted_iota(jnp.int32, (TM, TM), 1)
        same = (r // DN_CHUNK) == (c // DN_CHUNK)
        tri_f = jnp.where(jnp.logical_and(same, c <= r), 1.0, 0.0).astype(F32)
        tri_b = jnp.where(jnp.logical_and(same, c >= r), 1.0, 0.0).astype(F32)
        g_f = jnp.dot(tri_f, g, preferred_element_type=F32, precision=lax.Precision.HIGHEST)
        g_b = jnp.dot(tri_b, g, preferred_element_type=F32, precision=lax.Precision.HIGHEST)
        gc = jnp.where(lane < N_HEAD, g_f, g_b)
        gb_ref[...] = jnp.where(lane < 2 * N_HEAD, gc, bt)


def _dn_prep_call(qkv, ab, conv_w, a_log, dt_bias):
    B, T, W = qkv.shape
    nt = T // TM
    hb = TM // 16
    last16 = T // 16 - 1
    al = jnp.zeros((1, 128), F32).at[0, :2 * N_HEAD].set(a_log.reshape(-1))
    dtb = jnp.zeros((1, 128), F32).at[0, :2 * N_HEAD].set(dt_bias.reshape(-1))
    return pl.pallas_call(
        functools.partial(_dn_prep_kernel, nt=nt),
        grid=(B, nt, 3),
        in_specs=[
            pl.BlockSpec((None, TM, MIX_W), lambda b, i, j: (b, i, j)),
            pl.BlockSpec((None, 16, MIX_W), lambda b, i, j: (b, jnp.maximum(i * hb - 1, 0), j)),
            pl.BlockSpec((None, 16, MIX_W), lambda b, i, j: (b, jnp.minimum((i + 1) * hb, last16), j)),
            pl.BlockSpec((None, TM, 128), lambda b, i, j: (b, i, 0)),
            pl.BlockSpec((4, MIX_W), lambda b, i, j: (0, j)),
            pl.BlockSpec((1, 128), lambda b, i, j: (0, 0)),
            pl.BlockSpec((1, 128), lambda b, i, j: (0, 0)),
        ],
        out_specs=[
            pl.BlockSpec((None, TM, MIX_W), lambda b, i, j: (b, i, j)),
            pl.BlockSpec((None, TM, 128), lambda b, i, j: (b, i, 0)),
        ],
        out_shape=[jax.ShapeDtypeStruct((B, T, W), BF16), jax.ShapeDtypeStruct((B, T, 128), F32)],
        compiler_params=_cparams(("arbitrary", "arbitrary", "arbitrary")),
        name="dn_prep",
    )(qkv, qkv, qkv, ab, conv_w, al, dtb)


def _mm(a, b):
    return _dot(a.astype(BF16), b.astype(BF16))


def _solve_unit_tri(a, blk, rhs):
    d = jnp.where(blk, a, 0.0)
    lo = a - d
    p = _mm(d, d)
    n = -d
    for m in range(3):
        n = n + p + _mm(n, p)
        if m < 2:
            p = _mm(p, p)
    x0 = rhs + _mm(n, rhs)
    mt = lo + _mm(n, lo)
    x = x0
    for _ in range(3):
        x = x0 - _mm(mt, x)
    return x


def _dn_main_kernel(q_ref, k_ref, v_ref, gb_ref, gbt_ref, o_ref, s_ref, *, n_chunks, n_ctx_chunks):
    C = DN_CHUNK
    o_ref[...] = jnp.zeros_like(o_ref)
    s_ref[...] = jnp.zeros_like(s_ref)
    r = lax.broadcasted_iota(jnp.int32, (C, C), 0)
    c = lax.broadcasted_iota(jnp.int32, (C, C), 1)
    incl = (c <= r, c >= r)
    strict = (c < r, c > r)
    blk = (r // 16) == (c // 16)

    def step(s, carry):
        cf = s
        cb = jnp.where(s < n_ctx_chunks, n_ctx_chunks - 1 - s, n_chunks + n_ctx_chunks - 1 - s)
        for d, ci in ((0, cf), (1, cb)):
            rows = pl.ds(pl.multiple_of(ci * C, C), C)
            gb = gb_ref[rows, :]
            gbt = gbt_ref[ci]
            for hh in range(2):
                cols = slice(hh * HEAD_W, (hh + 1) * HEAD_W)
                q = q_ref[rows, cols]
                k = k_ref[rows, cols]
                v = v_ref[rows, cols]
                lg, lb = d * 2 + hh, 4 + d * 2 + hh
                gc_col = gb[:, lg:lg + 1]
                bt_col = gb[:, lb:lb + 1]
                gc_row = gbt[lg:lg + 1, :]
                gtot = gc_row[:, C - 1:C] if d == 0 else gc_row[:, 0:1]
                diff = gc_col - gc_row
                dec = jnp.where(incl[d], jnp.exp(jnp.where(incl[d], diff, 0.0)), 0.0)
                kk = _dot_nt(k, k)
                qk0 = _dot_nt(q, k)
                a = jnp.where(strict[d], kk * dec, 0.0) * bt_col
                eg = jnp.exp(gc_col)
                kf = k.astype(F32)
                rhs = jnp.concatenate([kf * (bt_col * eg), v.astype(F32) * bt_col], axis=1)
                wu = _solve_unit_tri(a, blk, rhs)
                w = wu[:, :HEAD_W]
                u = wu[:, HEAD_W:]
                qk = jnp.where(incl[d], qk0 * dec, 0.0)
                q_in = q.astype(F32) * eg
                ko = kf * jnp.exp(gtot - gc_col)
                ge = jnp.exp(gtot)
                si = d * 2 + hh
                st = s_ref[si]
                sb = st.astype(BF16)
                v_new = u - _dot(w.astype(BF16), sb)
                vb = v_new.astype(BF16)
                o = _dot(q_in.astype(BF16), sb) + _dot(qk.astype(BF16), vb)
                s_ref[si] = ge * st + _dot_tn(ko.astype(BF16), vb)
                o_ref[rows, cols] += o
        return carry

    lax.fori_loop(0, n_chunks, step, 0)


def _dn_main_call(qkvn, gb_pair, gbt_pair):
    B, T, _ = qkvn.shape
    n_chunks = T // DN_CHUNK
    pw = 2 * HEAD_W
    kern = functools.partial(_dn_main_kernel, n_chunks=n_chunks, n_ctx_chunks=CTX_LEN // DN_CHUNK)
    return pl.pallas_call(
        kern,
        grid=(B, 2),
        in_specs=[
            pl.BlockSpec((None, T, pw), lambda b, p: (b, 0, p)),
            pl.BlockSpec((None, T, pw), lambda b, p: (b, 0, 2 + p)),
            pl.BlockSpec((None, T, pw), lambda b, p: (b, 0, 4 + p)),
            pl.BlockSpec((None, None, T, 128), lambda b, p: (b, p, 0, 0)),
            pl.BlockSpec((None, None, n_chunks, 8, DN_CHUNK), lambda b, p: (b, p, 0, 0, 0)),
        ],
        out_specs=pl.BlockSpec((None, T, pw), lambda b, p: (b, 0, p)),
        out_shape=jax.ShapeDtypeStruct((B, T, MIX_W), F32),
        scratch_shapes=[pltpu.VMEM((4, HEAD_W, HEAD_W), F32)],
        compiler_params=_cparams(("arbitrary", "arbitrary")),
        name="dn_main",
    )(qkvn, qkvn, qkvn, gb_pair, gbt_pair)


def _gelu_tanh(x):
    return 0.5 * x * (1.0 + jnp.tanh(math.sqrt(2.0 / math.pi) * (x + 0.044715 * (x * x * x))))


def _lru_kernel(x_ref, y_ref, cw_ref, cb_ref, wg_ref, bg_ref, lam_ref, o_ref, hf_ref, *, nt):
    CW = x_ref.shape[1]
    T = x_ref.shape[0]
    G = TM // 8
    row = lax.broadcasted_iota(jnp.int32, (TM, CW), 0)
    sub = row % 8

    def coeffs(t, d):
        base = pl.multiple_of(t * TM, TM)
        x = x_ref[pl.ds(base, TM), :]
        xp = x_ref[pl.ds(pl.multiple_of(jnp.maximum(base - 8, 0), 8), 8), :]
        xn = x_ref[pl.ds(pl.multiple_of(jnp.minimum(base + TM, T - 8), 8), 8), :]
        vp, vn = _halo_valid(t, nt)
        xc = _dwconv_tile(x, xp[7:8, :] * vp, xn[0:1, :] * vn, xn[1:2, :] * vn, cw_ref[...], row) + cb_ref[...]
        z = _dot(xc.astype(BF16), wg_ref[:, d * 2 * CW:(d + 1) * 2 * CW]) + bg_ref[:, d * 2 * CW:(d + 1) * 2 * CW]
        rg = _sigmoid(z[:, :CW])
        ig = _sigmoid(z[:, CW:])
        log_a = (-LRU_C * _softplus(-lam_ref[d:d + 1, :])) * rg
        a = jnp.exp(log_a)
        b = jnp.sqrt(1.0 - a * a) * (ig * xc)
        return a, b

    def scan_tile(a, b, h_in, d):
        for s in (1, 2, 4):
            if d == 0:
                a_s, b_s, valid = pltpu.roll(a, s, 0), pltpu.roll(b, s, 0), sub >= s
            else:
                a_s, b_s, valid = pltpu.roll(a, TM - s, 0), pltpu.roll(b, TM - s, 0), sub < 8 - s
            b = jnp.where(valid, a * b_s + b, b)
            a = jnp.where(valid, a * a_s, a)
        hs = [None] * G
        h = h_in
        for g in (range(G) if d == 0 else range(G - 1, -1, -1)):
            hg = b[g * 8:(g + 1) * 8, :] + a[g * 8:(g + 1) * 8, :] * h
            hs[g] = hg
            h = hg[7:8, :] if d == 0 else hg[0:1, :]
        return jnp.concatenate(hs, axis=0), h

    def fwd(t, h):
        a, b = coeffs(t, 0)
        hf, h = scan_tile(a, b, h, 0)
        hf_ref[pl.ds(pl.multiple_of(t * TM, TM), TM), :] = hf
        return h

    lax.fori_loop(0, nt, fwd, jnp.zeros((1, CW), F32))

    def bwd(s, h):
        t = jnp.where(s == 0, 0, nt - s)
        a, b = coeffs(t, 1)
        hb, h = scan_tile(a, b, h, 1)
        rows = pl.ds(pl.multiple_of(t * TM, TM), TM)
        y = y_ref[rows, :].astype(F32)
        o_ref[rows, :] = ((hf_ref[rows, :] + hb) * _gelu_tanh(y)).astype(o_ref.dtype)
        return h

    lax.fori_loop(0, nt, bwd, jnp.zeros((1, CW), F32))


def _lru_call(x, y, conv_w, conv_b, wg, bg, lam):
    B, T, W = x.shape
    cw = W // 2
    nt = T // TM
    return pl.pallas_call(
        functools.partial(_lru_kernel, nt=nt),
        grid=(B, 2),
        in_specs=[
            pl.BlockSpec((None, T, cw), lambda b, p: (b, 0, p)),
            pl.BlockSpec((None, T, cw), lambda b, p: (b, 0, p)),
            pl.BlockSpec((4, cw), lambda b, p: (0, p)),
            pl.BlockSpec((1, cw), lambda b, p: (0, p)),
            pl.BlockSpec((None, cw, 4 * cw), lambda b, p: (p, 0, 0)),
            pl.BlockSpec((None, 1, 4 * cw), lambda b, p: (p, 0, 0)),
            pl.BlockSpec((2, cw), lambda b, p: (0, p)),
        ],
        out_specs=pl.BlockSpec((None, T, cw), lambda b, p: (b, 0, p)),
        out_shape=jax.ShapeDtypeStruct((B, T, W), BF16),
        scratch_shapes=[pltpu.VMEM((T, cw), F32)],
        compiler_params=_cparams(("arbitrary", "arbitrary")),
        name="rglru",
    )(x, y, conv_w, conv_b, wg, bg, lam)


def _rope(x, cs, sn, first_half):
    swapped = jnp.where(first_half, pltpu.roll(x, 112, 1), pltpu.roll(x, 16, 1))
    return x * cs + swapped * sn


def _attn_kernel(q_ref, k_ref, v_ref, cq_ref, sq_ref, ck_ref, sk_ref, lv_ref, nw_ref, o_ref, kr_ref,
                 *, lam_init, q_off, nt):
    qi = pl.program_id(2)
    lane = lax.broadcasted_iota(jnp.int32, (TM, HEAD_W), 1)
    first_half = (lane % 32) < 16

    @pl.when(qi == 0)
    def _():
        def rope_k(t, carry):
            rows = pl.ds(pl.multiple_of(t * TM, TM), TM)
            kr_ref[rows, :] = _rope(k_ref[rows, :].astype(F32), ck_ref[rows, :], sk_ref[rows, :],
                                    first_half).astype(BF16)
            return carry
        lax.fori_loop(0, nt, rope_k, 0)

    q = _rope(q_ref[...].astype(F32), cq_ref[...], sq_ref[...], first_half) * (64 ** -0.5)
    qq = jnp.concatenate([jnp.where(lane < 64, q, 0.0), jnp.where(lane >= 64, q, 0.0)], axis=0).astype(BF16)
    n_kv = jnp.where(qi + q_off == 0, 1, nt)

    def body(j, carry):
        m, l, acc = carry
        rows = pl.ds(pl.multiple_of(j * TM, TM), TM)
        s = _dot_nt(qq, kr_ref[rows, :])
        m_new = jnp.maximum(m, jnp.max(s, axis=-1, keepdims=True))
        alpha = jnp.exp(m - m_new)
        p = jnp.exp(s - m_new)
        l = alpha * l + jnp.sum(p, axis=-1, keepdims=True)
        acc = alpha * acc + _dot(p.astype(BF16), v_ref[rows, :])
        return m_new, l, acc

    m0 = jnp.full((2 * TM, 1), -1e30, F32)
    l0 = jnp.zeros((2 * TM, 1), F32)
    a0 = jnp.zeros((2 * TM, HEAD_W), F32)
    m, l, acc = lax.fori_loop(0, n_kv, body, (m0, l0, a0))
    on = acc * (1.0 / l)
    lv = lv_ref[...]
    lam = (jnp.exp(jnp.sum(lv[0:1, :] * lv[1:2, :], axis=-1, keepdims=True))
           - jnp.exp(jnp.sum(lv[2:3, :] * lv[3:4, :], axis=-1, keepdims=True)) + lam_init)
    o = on[:TM, :] - lam * on[TM:, :]
    ms = jnp.mean(o * o, axis=-1, keepdims=True)
    o_ref[...] = (o * lax.rsqrt(ms + 1e-5) * nw_ref[...] * (1.0 - lam_init)).astype(o_ref.dtype)


def _attn_call(qkv, cos_t, sin_t, lam_vecs, sub_norm, lam_init, q_off):
    B, T, _ = qkv.shape
    nt = T // TM
    nq = nt - q_off
    kern = functools.partial(_attn_kernel, lam_init=lam_init, q_off=q_off, nt=nt)
    return pl.pallas_call(
        kern,
        grid=(B, N_HEAD, nq),
        in_specs=[
            pl.BlockSpec((None, TM, HEAD_W), lambda b, h, i: (b, i + q_off, h)),
            pl.BlockSpec((None, T, HEAD_W), lambda b, h, i: (b, 0, N_HEAD + h)),
            pl.BlockSpec((None, T, HEAD_W), lambda b, h, i: (b, 0, 2 * N_HEAD + h)),
            pl.BlockSpec((TM, HEAD_W), lambda b, h, i: (i + q_off, 0)),
            pl.BlockSpec((TM, HEAD_W), lambda b, h, i: (i + q_off, 0)),
            pl.BlockSpec((T, HEAD_W), lambda b, h, i: (0, 0)),
            pl.BlockSpec((T, HEAD_W), lambda b, h, i: (0, 0)),
            pl.BlockSpec((4, 64), lambda b, h, i: (0, 0)),
            pl.BlockSpec((1, HEAD_W), lambda b, h, i: (0, 0)),
        ],
        out_specs=pl.BlockSpec((None, TM, HEAD_W), lambda b, h, i: (b, i + q_off, h)),
        out_shape=jax.ShapeDtypeStruct((B, T, MIX_W), BF16),
        scratch_shapes=[pltpu.VMEM((T, HEAD_W), BF16)],
        compiler_params=_cparams(("arbitrary", "arbitrary", "arbitrary")),
        name="diff_attn",
    )(qkv, qkv, qkv, cos_t, sin_t, cos_t, sin_t, lam_vecs, sub_norm)


def _merge_kernel(h_ref, mod_ref, dno_ref, z_ref, yb_ref, yc_ref, g_ref, dnw_ref, wb_ref, wo_ref, o_ref):
    dnw = dnw_ref[...]
    ya_parts = []
    for hh in range(N_HEAD):
        cols = slice(hh * HEAD_W, (hh + 1) * HEAD_W)
        o = dno_ref[:, cols]
        ms = jnp.mean(o * o, axis=-1, keepdims=True)
        z = z_ref[:, cols].astype(F32)
        ya_parts.append(((o * lax.rsqrt(ms + 1e-6) * dnw) * (z * _sigmoid(z))).astype(BF16))
    ys = (jnp.concatenate(ya_parts, axis=1), yb_ref[...], yc_ref[...])
    mix = None
    for n in range(3):
        up = _dot(ys[n], wb_ref[n])
        term = _sigmoid(g_ref[:, n * D_MODEL:(n + 1) * D_MODEL].astype(F32)) * up
        mix = term if mix is None else mix + term
    y = _dot(mix.astype(BF16), wo_ref[...])
    o_ref[...] = h_ref[...] + mod_ref[2:3, :] * y


def _merge_call(h, mod, dno, z, yb, yc, gates, dn_norm, wb, wo, t_off):
    B, T, _ = h.shape
    nt = T // TM - t_off
    row = lambda w: pl.BlockSpec((None, TM, w), lambda b, i: (b, i + t_off, 0))
    return pl.pallas_call(
        _merge_kernel,
        grid=(B, nt),
        in_specs=[
            row(D_MODEL),
            pl.BlockSpec((None, None, 6, D_MODEL), lambda b, i: _mod_index(b, i + t_off)),
            row(MIX_W), row(MIX_W), row(MIX_W), row(MIX_W), row(3 * D_MODEL),
            pl.BlockSpec((1, HEAD_W), lambda b, i: (0, 0)),
            pl.BlockSpec((3, MIX_W, D_MODEL), lambda b, i: (0, 0, 0)),
            pl.BlockSpec((D_MODEL, D_MODEL), lambda b, i: (0, 0)),
        ],
        out_specs=pl.BlockSpec((None, TM, D_MODEL), lambda b, i: (b, i, 0)),
        out_shape=jax.ShapeDtypeStruct((B, nt * TM, D_MODEL), F32),
        compiler_params=_cparams(("arbitrary", "arbitrary")),
        name="merge",
    )(h, mod, dno, z, yb, yc, gates, dn_norm, wb, wo)


def _ffn_kernel(h_ref, mod_ref, nw_ref, wg_ref, wu_ref, wd_ref, fw_ref, o_ref, *, final):
    h = h_ref[...]
    x = _norm_modulate(h, nw_ref[...], mod_ref[3:4, :], mod_ref[4:5, :]).astype(BF16)
    acc = None
    for c in range(0, D_FF, FF_CHUNK):
        g = _dot(x, wg_ref[:, c:c + FF_CHUNK])
        u = _dot(x, wu_ref[:, c:c + FF_CHUNK])
        part = _dot(((g * _sigmoid(g)) * u).astype(BF16), wd_ref[c:c + FF_CHUNK, :])
        acc = part if acc is None else acc + part
    out = h + mod_ref[5:6, :] * acc
    if final:
        ms = jnp.mean(out * out, axis=-1, keepdims=True)
        out = out * lax.rsqrt(ms + 1e-6) * fw_ref[...]
    o_ref[...] = out


def _ffn_call(h, mod, nw, wg, wu, wd, fw, seg_off, final):
    B, T, _ = h.shape
    nt = T // TM
    return pl.pallas_call(
        functools.partial(_ffn_kernel, final=final),
        grid=(B, nt),
        in_specs=[
            pl.BlockSpec((None, TM, D_MODEL), lambda b, i: (b, i, 0)),
            pl.BlockSpec((None, None, 6, D_MODEL), lambda b, i: _mod_index(b, i + seg_off)),
            pl.BlockSpec((1, D_MODEL), lambda b, i: (0, 0)),
            pl.BlockSpec((D_MODEL, D_FF), lambda b, i: (0, 0)),
            pl.BlockSpec((D_MODEL, D_FF), lambda b, i: (0, 0)),
            pl.BlockSpec((D_FF, D_MODEL), lambda b, i: (0, 0)),
            pl.BlockSpec((1, D_MODEL), lambda b, i: (0, 0)),
        ],
        out_specs=pl.BlockSpec((None, TM, D_MODEL), lambda b, i: (b, i, 0)),
        out_shape=jax.ShapeDtypeStruct((B, T, D_MODEL), F32),
        compiler_params=_cparams(("arbitrary", "arbitrary")),
        name="ffn",
    )(h, mod, nw, wg, wu, wd, fw)


def _reorder_w_in(w):
    parts = [w[:, 0:1536], w[:, 1536:2048], w[:, 2064:2576], w[:, 2576:3088], w[:, 3088:4624], w[:, 4624:7696],
             w[:, 2048:2064], jnp.zeros((w.shape[0], IN_COLS_PAD - 7696), w.dtype)]
    return jnp.concatenate(parts, axis=1).astype(BF16)


def _block_diag(w):
    g, n, _ = w.shape
    return jnp.einsum('gij,gh->gihj', w, jnp.eye(g, dtype=w.dtype)).reshape(g * n, g * n)


def _lru_gate_weights(wa, ba, wi, bi):
    half = LRU_G // 2
    ws, bs = [], []
    for p in range(2):
        grp = slice(p * half, (p + 1) * half)
        ch = slice(p * MIX_W // 2, (p + 1) * MIX_W // 2)
        cols, bcols = [], []
        for d in range(2):
            cols += [_block_diag(wa[d, grp]), _block_diag(wi[d, grp])]
            bcols += [ba[d, ch], bi[d, ch]]
        ws.append(jnp.concatenate(cols, axis=1))
        bs.append(jnp.concatenate(bcols)[None, :])
    return jnp.stack(ws).astype(BF16), jnp.stack(bs)


def _rope_tables(seq):
    rows = seq // GRID_W
    rowp = jnp.repeat(jnp.arange(rows, dtype=F32), GRID_W)
    colp = jnp.tile(jnp.arange(GRID_W, dtype=F32), rows)
    inv = ROPE_BASE ** (-jnp.arange(ROPE_NF, dtype=F32) / ROPE_NF)
    ar, ac = rowp[:, None] * inv, colp[:, None] * inv
    cs = jnp.concatenate([jnp.cos(ar), jnp.cos(ar), jnp.cos(ac), jnp.cos(ac)], axis=1)
    sn = jnp.concatenate([-jnp.sin(ar), jnp.sin(ar), -jnp.sin(ac), jnp.sin(ac)], axis=1)
    cs = jnp.concatenate([jnp.ones((CTX_LEN, 64), F32), cs], axis=0)
    sn = jnp.concatenate([jnp.zeros((CTX_LEN, 64), F32), sn], axis=0)
    return jnp.tile(cs, (1, 2)), jnp.tile(sn, (1, 2))


def _pair_gates(gb):
    B, T, _ = gb.shape
    pairs = []
    for p in range(2):
        idx = [d * 4 + 2 * p + hh for d in range(2) for hh in range(2)]
        idx = idx + [8 + i for i in idx]
        pairs.append(gb[:, :, jnp.array(idx)])
    g8 = jnp.stack(pairs, axis=1)
    g_tok = jnp.pad(g8, ((0, 0), (0, 0), (0, 0), (0, 120)))
    g_chn = g8.reshape(B, 2, T // DN_CHUNK, DN_CHUNK, 8).transpose(0, 1, 2, 4, 3)
    return g_tok, g_chn


def kernel(x, c, ctx, c_ctx, w_mod, b_mod, norm_mix, norm_ffn, w_in, dn_conv, dn_a_log, dn_dt_bias, dn_norm, lru_conv_w, lru_conv_b, lru_wa, lru_ba, lru_wi, lru_bi, lru_lambda, da_lambda, da_norm, w_branch, w_out, w_ffn_gate, w_ffn_up, w_ffn_down, norm_final):
    B, S, D = x.shape
    depth = w_mod.shape[0]
    rows = jnp.concatenate([c, c_ctx[None, :], jnp.zeros((16 - B - 1, D), F32)], axis=0)
    mod = _mod_call(rows, w_mod, b_mod).reshape(depth, 16, 6, D)
    cos_t, sin_t = _rope_tables(S)
    h = jnp.concatenate([ctx, x], axis=1)
    for l in range(depth):
        last = l == depth - 1
        t_off = 1 if last else 0
        lam_init = 0.8 - 0.6 * math.exp(-0.3 * l)
        mod_l = jnp.stack([jnp.broadcast_to(mod[l, B][None], (B, 6, D)), mod[l, :B]], axis=1)
        dn_qkv, dn_z, lru_x, lru_y, da_qkv, gates, ab = _inproj_call(
            h, mod_l, norm_mix[l][None, :], _reorder_w_in(w_in[l]))
        dn_qkvn, gb = _dn_prep_call(dn_qkv, ab, dn_conv[l], dn_a_log[l], dn_dt_bias[l])
        g_tok, g_chn = _pair_gates(gb)
        dn_o = _dn_main_call(dn_qkvn, g_tok, g_chn)
        wg, bg = _lru_gate_weights(lru_wa[l], lru_ba[l], lru_wi[l], lru_bi[l])
        yb = _lru_call(lru_x, lru_y, lru_conv_w[l], lru_conv_b[l][None, :], wg, bg, lru_lambda[l])
        yc = _attn_call(da_qkv, cos_t, sin_t, da_lambda[l], da_norm[l][None, :], lam_init, t_off)
        h = _merge_call(h, mod_l, dn_o, dn_z, yb, yc, gates, dn_norm[l][None, :],
                        w_branch[l].astype(BF16), w_out[l].astype(BF16), t_off)
        h = _ffn_call(h, mod_l, norm_ffn[l][None, :], w_ffn_gate[l].astype(BF16), w_ffn_up[l].astype(BF16),
                      w_ffn_down[l].astype(BF16), norm_final[None, :], t_off, last)
    return h
```

```python
import functools
import math

import jax
import jax.numpy as jnp
from jax import lax
from jax.experimental import pallas as pl
from jax.experimental.pallas import tpu as pltpu

F32 = jnp.float32
BF16 = jnp.bfloat16

D_MODEL = 1024
CTX_LEN = 256
GRID_W = 64
MIX_W = 512
N_HEAD = 4
HEAD_W = 128
DN_CHUNK = 64
LRU_C = 8.0
LRU_G = 8
ROPE_BASE = 10000.0
ROPE_NF = 16
D_FF = 2816
TM = 256
FF_CHUNK = 256
VT_ROWS = HEAD_W + 16
ATTN_HP = 2
VMEM_LIMIT = 56 * 1024 * 1024

C_DNQKV, C_DNZ, C_LRUX, C_LRUY, C_DAQKV, C_GATE, C_AB = 0, 1536, 2048, 2560, 3072, 4608, 7680
IN_COLS_PAD = 7808


def _cparams(sem):
    return pltpu.CompilerParams(dimension_semantics=sem, vmem_limit_bytes=VMEM_LIMIT)


def _sigmoid(x):
    return 1.0 / (1.0 + jnp.exp(-x))


def _softplus(x):
    return jnp.maximum(x, 0.0) + jnp.log(1.0 + jnp.exp(-jnp.abs(x)))


def _dot(a, b):
    return jnp.dot(a, b, preferred_element_type=F32)


def _dot_nt(a, b):
    return lax.dot_general(a, b, (((1,), (1,)), ((), ())), preferred_element_type=F32)


def _dot_tn(a, b):
    return lax.dot_general(a, b, (((0,), (0,)), ((), ())), preferred_element_type=F32)


def _mod_kernel(s_ref, w_ref, b_ref, o_ref):
    s = s_ref[...]
    s = s * _sigmoid(s)
    o_ref[...] = jnp.dot(s, w_ref[...], preferred_element_type=F32,
                         precision=lax.Precision.HIGHEST) + b_ref[...]


def _mod_call(rows, w_mod, b_mod):
    depth = w_mod.shape[0]
    n6 = w_mod.shape[2]
    tn = 1024
    return pl.pallas_call(
        _mod_kernel,
        grid=(depth, n6 // tn),
        in_specs=[
            pl.BlockSpec((16, D_MODEL), lambda l, j: (0, 0)),
            pl.BlockSpec((None, D_MODEL, tn), lambda l, j: (l, 0, j)),
            pl.BlockSpec((None, 1, tn), lambda l, j: (l, 0, j)),
        ],
        out_specs=pl.BlockSpec((None, 16, tn), lambda l, j: (l, 0, j)),
        out_shape=jax.ShapeDtypeStruct((depth, 16, n6), F32),
        compiler_params=_cparams(("arbitrary", "arbitrary")),
        name="adaln_mod",
    )(rows, w_mod, b_mod.reshape(depth, 1, n6))


def _mod_index(b, i):
    return (b, jnp.minimum(i, 1), 0, 0)


def _norm_modulate(x, nw, shift, scale, eps=1e-6):
    ms = jnp.mean(x * x, axis=-1, keepdims=True)
    y = x * lax.rsqrt(ms + eps) * nw
    return y * (1.0 + scale) + shift


_INPROJ_SEGS = (
    (C_DNQKV, 1536), (C_DNZ, 512), (C_LRUX, 512), (C_LRUY, 512), (C_DAQKV, 1536), (C_GATE, 3072), (C_AB, 128))


def _inproj_kernel(h_ref, mod_ref, nw_ref, w_ref, *out_refs):
    u = _norm_modulate(h_ref[...], nw_ref[...], mod_ref[0:1, :], mod_ref[1:2, :]).astype(BF16)
    for (c0, width), o_ref in zip(_INPROJ_SEGS, out_refs):
        step = min(width, 512)
        for c in range(0, width, step):
            o_ref[:, c:c + step] = _dot(u, w_ref[:, c0 + c:c0 + c + step]).astype(o_ref.dtype)


def _inproj_call(h, mod, nw, w):
    B, T, _ = h.shape
    nt = T // TM
    dts = (BF16, BF16, F32, BF16, BF16, BF16, F32)
    return pl.pallas_call(
        _inproj_kernel,
        grid=(B, nt),
        in_specs=[
            pl.BlockSpec((None, TM, D_MODEL), lambda b, i: (b, i, 0)),
            pl.BlockSpec((None, None, 6, D_MODEL), _mod_index),
            pl.BlockSpec((1, D_MODEL), lambda b, i: (0, 0)),
            pl.BlockSpec((D_MODEL, IN_COLS_PAD), lambda b, i: (0, 0)),
        ],
        out_specs=[pl.BlockSpec((None, TM, wd), lambda b, i: (b, i, 0)) for _, wd in _INPROJ_SEGS],
        out_shape=[jax.ShapeDtypeStruct((B, T, wd), dt) for (_, wd), dt in zip(_INPROJ_SEGS, dts)],
        compiler_params=_cparams(("arbitrary", "arbitrary")),
        name="in_proj",
    )(h, mod, nw, w)


def _dwconv_tile(x, prev_last, next0, next1, w, row):
    R = x.shape[0]
    xm1 = jnp.where(row == 0, prev_last, pltpu.roll(x, 1, 0))
    xp1 = jnp.where(row == R - 1, next0, pltpu.roll(x, R - 1, 0))
    xp2 = jnp.where(row == R - 2, next0, jnp.where(row == R - 1, next1, pltpu.roll(x, R - 2, 0)))
    return w[0:1, :] * xm1 + w[1:2, :] * x + w[2:3, :] * xp1 + w[3:4, :] * xp2


def _halo_valid(i, nt):
    vp = (i >= 2).astype(F32)
    vn = jnp.logical_and(i >= 1, i <= nt - 2).astype(F32)
    return vp, vn


def _mm(a, b):
    return _dot(a.astype(BF16), b.astype(BF16))


def _solve_unit_tri(a, blk, rhs):
    d = jnp.where(blk, a, 0.0)
    lo = a - d
    p = _mm(d, d)
    n = -d
    for m in range(3):
        n = n + p + _mm(n, p)
        if m < 2:
            p = _mm(p, p)
    x0 = rhs + _mm(n, rhs)
    mt = lo + _mm(n, lo)
    x = x0
    for _ in range(3):
        x = x0 - _mm(mt, x)
    return x


def _dn_chunk_kernel(x_ref, xp_ref, xn_ref, ab_ref, cw_ref, al_ref, dtb_ref,
                     wf_ref, uf_ref, qf_ref, kf_ref, qkf_ref, wb_ref, ub_ref, qb_ref, kb_ref, qkb_ref, ge_ref,
                     s_ref, gc_ref, bt_ref, gct_ref, gt_ref, *, nt):
    i = pl.program_id(1)
    C = DN_CHUNK
    hi = lax.Precision.HIGHEST
    vp, vn = _halo_valid(i, nt)
    rowc = lax.broadcasted_iota(jnp.int32, (TM, HEAD_W), 0)
    for cb in range(3 * N_HEAD):
        cols = slice(cb * HEAD_W, (cb + 1) * HEAD_W)
        xp = xp_ref[:, cols].astype(F32)
        xn = xn_ref[:, cols].astype(F32)
        y = _dwconv_tile(x_ref[:, cols].astype(F32), xp[15:16, :] * vp, xn[0:1, :] * vn, xn[1:2, :] * vn,
                         cw_ref[:, cols], rowc)
        s = y * _sigmoid(y)
        if cb < 2 * N_HEAD:
            ss = jnp.sum(s * s, axis=-1, keepdims=True)
            s = s * (lax.rsqrt(ss + 1e-6) * (HEAD_W ** -0.5 if cb < N_HEAD else 1.0))
        s_ref[cb] = s

    def masks():
        r = lax.broadcasted_iota(jnp.int32, (TM, TM), 0)
        c = lax.broadcasted_iota(jnp.int32, (TM, TM), 1)
        same = (r // C) == (c // C)
        return r, c, same

    ab = ab_ref[...]
    lane = lax.broadcasted_iota(jnp.int32, (TM, 128), 1)
    g = -jnp.exp(al_ref[...]) * _softplus(ab + dtb_ref[...])
    r, c, same = masks()
    one_f = jnp.where(jnp.logical_and(same, c <= r), 1.0, 0.0).astype(F32)
    one_b = jnp.where(jnp.logical_and(same, c >= r), 1.0, 0.0).astype(F32)
    one_s = jnp.where(same, 1.0, 0.0).astype(F32)
    g_f = jnp.dot(one_f, g, preferred_element_type=F32, precision=hi)
    g_b = jnp.dot(one_b, g, preferred_element_type=F32, precision=hi)
    gtot = jnp.dot(one_s, g, preferred_element_type=F32, precision=hi)
    gc = jnp.where(lane < N_HEAD, g_f, g_b)
    gc_ref[...] = gc
    bt_ref[...] = _sigmoid(ab)
    gt_ref[...] = gtot
    gct_ref[...] = gc.T
    ge_ref[...] = jnp.exp(gtot)
    out_refs = ((wf_ref, uf_ref, qf_ref, kf_ref, qkf_ref), (wb_ref, ub_ref, qb_ref, kb_ref, qkb_ref))

    blk = (r // 16) == (c // 16)
    sub8 = lax.broadcasted_iota(jnp.int32, (8, TM), 0)
    for h in range(N_HEAD):
        q = s_ref[h]
        k = s_ref[N_HEAD + h]
        v = s_ref[2 * N_HEAD + h]
        kb16 = k.astype(BF16)
        kk = _dot_nt(kb16, kb16)
        qk0 = _dot_nt(q.astype(BF16), kb16)
        for d in range(2):
            idx = d * N_HEAD + h
            incl = jnp.logical_and(same, c <= r) if d == 0 else jnp.logical_and(same, c >= r)
            strict = jnp.logical_and(same, c < r) if d == 0 else jnp.logical_and(same, c > r)
            sel = lane == idx
            gc_col = jnp.sum(jnp.where(sel, gc_ref[...], 0.0), axis=1, keepdims=True)
            gt_col = jnp.sum(jnp.where(sel, gt_ref[...], 0.0), axis=1, keepdims=True)
            bt_col = jnp.sum(jnp.where(lane == idx + 2 * N_HEAD, bt_ref[...], 0.0), axis=1, keepdims=True)
            gc_row = jnp.sum(jnp.where(sub8 == idx, gct_ref[0:8, :], 0.0), axis=0, keepdims=True)
            diff = gc_col - gc_row
            dec = jnp.where(incl, jnp.exp(jnp.where(incl, diff, 0.0)), 0.0)
            a = jnp.where(strict, kk * dec, 0.0) * bt_col
            eg = jnp.exp(gc_col)
            rhs = jnp.concatenate([k * (bt_col * eg), v * bt_col], axis=1)
            wu = _solve_unit_tri(a, blk, rhs)
            w_r, u_r, q_r, k_r, qk_r = out_refs[d]
            w_r[h] = wu[:, :HEAD_W].astype(BF16)
            u_r[h] = wu[:, HEAD_W:]
            q_r[h] = (q * eg).astype(BF16)
            k_r[h] = (k * jnp.exp(gt_col - gc_col)).astype(BF16)
            qk_r[h] = jnp.where(incl, qk0 * dec, 0.0).astype(BF16)


def _dn_chunk_call(qkv, ab, conv_w, a_log, dt_bias):
    B, T, W = qkv.shape
    nt = T // TM
    hb = TM // 16
    last16 = T // 16 - 1
    al = jnp.zeros((1, 128), F32).at[0, :2 * N_HEAD].set(a_log.reshape(-1))
    dtb = jnp.zeros((1, 128), F32).at[0, :2 * N_HEAD].set(dt_bias.reshape(-1))
    hspec = lambda w: pl.BlockSpec((None, N_HEAD, TM, w), lambda b, i: (b, 0, i, 0))
    hshape = lambda w, dt: jax.ShapeDtypeStruct((B, N_HEAD, T, w), dt)
    per_dir_specs = [hspec(HEAD_W), hspec(HEAD_W), hspec(HEAD_W), hspec(HEAD_W), hspec(TM)]
    per_dir_shapes = [hshape(HEAD_W, BF16), hshape(HEAD_W, F32), hshape(HEAD_W, BF16), hshape(HEAD_W, BF16),
                      hshape(TM, BF16)]
    return pl.pallas_call(
        functools.partial(_dn_chunk_kernel, nt=nt),
        grid=(B, nt),
        in_specs=[
            pl.BlockSpec((None, TM, W), lambda b, i: (b, i, 0)),
            pl.BlockSpec((None, 16, W), lambda b, i: (b, jnp.maximum(i * hb - 1, 0), 0)),
            pl.BlockSpec((None, 16, W), lambda b, i: (b, jnp.minimum((i + 1) * hb, last16), 0)),
            pl.BlockSpec((None, TM, 128), lambda b, i: (b, i, 0)),
            pl.BlockSpec((4, W), lambda b, i: (0, 0)),
            pl.BlockSpec((1, 128), lambda b, i: (0, 0)),
            pl.BlockSpec((1, 128), lambda b, i: (0, 0)),
        ],
        out_specs=per_dir_specs + per_dir_specs + [pl.BlockSpec((None, TM, 128), lambda b, i: (b, i, 0))],
        out_shape=per_dir_shapes + per_dir_shapes + [jax.ShapeDtypeStruct((B, T, 128), F32)],
        scratch_shapes=[
            pltpu.VMEM((3 * N_HEAD, TM, HEAD_W), F32),
            pltpu.VMEM((TM, 128), F32), pltpu.VMEM((TM, 128), F32),
            pltpu.VMEM((128, TM), F32), pltpu.VMEM((TM, 128), F32),
        ],
        compiler_params=_cparams(("arbitrary", "arbitrary")),
        name="dn_chunk",
    )(qkv, qkv, qkv, ab, conv_w, al, dtb)


def _dn_seq_kernel(wf_ref, uf_ref, qf_ref, kf_ref, qkf_ref, gef_ref, wb_ref, ub_ref, qb_ref, kb_ref, qkb_ref,
                   geb_ref, of_ref, ob_ref, st_ref):
    C = DN_CHUNK
    nc = TM // C

    @pl.when(pl.program_id(1) == 0)
    def _():
        st_ref[...] = jnp.zeros_like(st_ref)

    ins = ((wf_ref, uf_ref, qf_ref, kf_ref, qkf_ref, gef_ref, of_ref),
           (wb_ref, ub_ref, qb_ref, kb_ref, qkb_ref, geb_ref, ob_ref))
    for ci in range(nc):
        for d in range(2):
            c = ci if d == 0 else nc - 1 - ci
            rows = slice(c * C, (c + 1) * C)
            w_r, u_r, q_r, k_r, qk_r, ge_r, o_r = ins[d]
            ge8 = ge_r[c * C:c * C + 8, :]
            for h in range(N_HEAD):
                si = d * N_HEAD + h
                st = st_ref[si]
                sb = st.astype(BF16)
                v_new = u_r[h, rows, :] - _dot(w_r[h, rows, :], sb)
                vb = v_new.astype(BF16)
                o_r[rows, h * HEAD_W:(h + 1) * HEAD_W] = (
                    _dot(q_r[h, rows, :], sb) + _dot(qk_r[h, rows, c * C:(c + 1) * C], vb))
                st_ref[si] = ge8[0:1, si:si + 1] * st + _dot_tn(k_r[h, rows, :], vb)


def _dn_seq_call(chunk_outs):
    wf, uf, qf, kf, qkf, wb, ub, qb, kb, qkb, ge = chunk_outs
    B, _, T, _ = wf.shape
    nt = T // TM
    tf = lambda s: s
    tb = lambda s: jnp.where(s == 0, 0, nt - s)
    hspec = lambda w, t: pl.BlockSpec((None, N_HEAD, TM, w), lambda b, s: (b, 0, t(s), 0))
    gspec = lambda t: pl.BlockSpec((None, TM, 128), lambda b, s: (b, t(s), 0))
    dir_specs = lambda t: [hspec(HEAD_W, t), hspec(HEAD_W, t), hspec(HEAD_W, t), hspec(HEAD_W, t), hspec(TM, t),
                           gspec(t)]
    ospec = lambda t: pl.BlockSpec((None, TM, MIX_W), lambda b, s: (b, t(s), 0))
    return pl.pallas_call(
        _dn_seq_kernel,
        grid=(B, nt),
        in_specs=dir_specs(tf) + dir_specs(tb),
        out_specs=[ospec(tf), ospec(tb)],
        out_shape=[jax.ShapeDtypeStruct((B, T, MIX_W), F32)] * 2,
        scratch_shapes=[pltpu.VMEM((2 * N_HEAD, HEAD_W, HEAD_W), F32)],
        compiler_params=_cparams(("arbitrary", "arbitrary")),
        name="dn_seq",
    )(wf, uf, qf, kf, qkf, ge, wb, ub, qb, kb, qkb, ge)


def _gelu_tanh(x):
    return 0.5 * x * (1.0 + jnp.tanh(math.sqrt(2.0 / math.pi) * (x + 0.044715 * (x * x * x))))


def _lru_kernel(x_ref, y_ref, cw_ref, cb_ref, wg_ref, bg_ref, lam_ref, o_ref, hf_ref, *, nt):
    CW = x_ref.shape[1]
    T = x_ref.shape[0]
    G = TM // 8
    row = lax.broadcasted_iota(jnp.int32, (TM, CW), 0)
    sub = row % 8

    def coeffs(t, d):
        base = pl.multiple_of(t * TM, TM)
        x = x_ref[pl.ds(base, TM), :]
        xp = x_ref[pl.ds(pl.multiple_of(jnp.maximum(base - 8, 0), 8), 8), :]
        xn = x_ref[pl.ds(pl.multiple_of(jnp.minimum(base + TM, T - 8), 8), 8), :]
        vp, vn = _halo_valid(t, nt)
        xc = _dwconv_tile(x, xp[7:8, :] * vp, xn[0:1, :] * vn, xn[1:2, :] * vn, cw_ref[...], row) + cb_ref[...]
        z = _dot(xc.astype(BF16), wg_ref[:, d * 2 * CW:(d + 1) * 2 * CW]) + bg_ref[:, d * 2 * CW:(d + 1) * 2 * CW]
        rg = _sigmoid(z[:, :CW])
        ig = _sigmoid(z[:, CW:])
        log_a = (-LRU_C * _softplus(-lam_ref[d:d + 1, :])) * rg
        a = jnp.exp(log_a)
        b = jnp.sqrt(1.0 - a * a) * (ig * xc)
        return a, b

    def scan_tile(a, b, h_in, d):
        for s in (1, 2, 4):
            if d == 0:
                a_s, b_s, valid = pltpu.roll(a, s, 0), pltpu.roll(b, s, 0), sub >= s
            else:
                a_s, b_s, valid = pltpu.roll(a, TM - s, 0), pltpu.roll(b, TM - s, 0), sub < 8 - s
            b = jnp.where(valid, a * b_s + b, b)
            a = jnp.where(valid, a * a_s, a)
        hs = [None] * G
        h = h_in
        for g in (range(G) if d == 0 else range(G - 1, -1, -1)):
            hg = b[g * 8:(g + 1) * 8, :] + a[g * 8:(g + 1) * 8, :] * h
            hs[g] = hg
            h = hg[7:8, :] if d == 0 else hg[0:1, :]
        return jnp.concatenate(hs, axis=0), h

    def fwd(t, h):
        a, b = coeffs(t, 0)
        hf, h = scan_tile(a, b, h, 0)
        hf_ref[pl.ds(pl.multiple_of(t * TM, TM), TM), :] = hf
        return h

    lax.fori_loop(0, nt, fwd, jnp.zeros((1, CW), F32))

    def bwd(s, h):
        t = jnp.where(s == 0, 0, nt - s)
        a, b = coeffs(t, 1)
        hb, h = scan_tile(a, b, h, 1)
        rows = pl.ds(pl.multiple_of(t * TM, TM), TM)
        y = y_ref[rows, :].astype(F32)
        o_ref[rows, :] = ((hf_ref[rows, :] + hb) * _gelu_tanh(y)).astype(o_ref.dtype)
        return h

    lax.fori_loop(0, nt, bwd, jnp.zeros((1, CW), F32))


def _lru_call(x, y, conv_w, conv_b, wg, bg, lam):
    B, T, W = x.shape
    cw = W // 2
    nt = T // TM
    return pl.pallas_call(
        functools.partial(_lru_kernel, nt=nt),
        grid=(B, 2),
        in_specs=[
            pl.BlockSpec((None, T, cw), lambda b, p: (b, 0, p)),
            pl.BlockSpec((None, T, cw), lambda b, p: (b, 0, p)),
            pl.BlockSpec((4, cw), lambda b, p: (0, p)),
            pl.BlockSpec((1, cw), lambda b, p: (0, p)),
            pl.BlockSpec((None, cw, 4 * cw), lambda b, p: (p, 0, 0)),
            pl.BlockSpec((None, 1, 4 * cw), lambda b, p: (p, 0, 0)),
            pl.BlockSpec((2, cw), lambda b, p: (0, p)),
        ],
        out_specs=pl.BlockSpec((None, T, cw), lambda b, p: (b, 0, p)),
        out_shape=jax.ShapeDtypeStruct((B, T, W), BF16),
        scratch_shapes=[pltpu.VMEM((T, cw), F32)],
        compiler_params=_cparams(("arbitrary", "arbitrary")),
        name="rglru",
    )(x, y, conv_w, conv_b, wg, bg, lam)


def _rope(x, cs, sn, first_half):
    swapped = jnp.where(first_half, pltpu.roll(x, 112, 1), pltpu.roll(x, 16, 1))
    return x * cs + swapped * sn


def _attn_kernel(q_ref, k_ref, v_ref, cq_ref, sq_ref, ck_ref, sk_ref, lv_ref, nw_ref, o_ref, kr_ref, vt_ref,
                 qt_ref, st0_ref, st1_ref, acc_ref, *, lam_init, q_off, nt):
    qi = pl.program_id(2)
    lane = lax.broadcasted_iota(jnp.int32, (TM, HEAD_W), 1)
    first_half = (lane % 32) < 16
    heads = [(hh, slice(hh * HEAD_W, (hh + 1) * HEAD_W)) for hh in range(ATTN_HP)]

    @pl.when(qi == 0)
    def _():
        extra = (lax.broadcasted_iota(jnp.int32, (VT_ROWS - HEAD_W, TM), 0) == 0).astype(BF16)

        def prep_kv(t, carry):
            rows = pl.ds(pl.multiple_of(t * TM, TM), TM)
            for hh, cols in heads:
                kr_ref[hh, rows, :] = _rope(k_ref[rows, cols].astype(F32), ck_ref[rows, :], sk_ref[rows, :],
                                            first_half).astype(BF16)
                vt_ref[hh, 0:HEAD_W, rows] = v_ref[rows, cols].astype(F32).T.astype(BF16)
                vt_ref[hh, HEAD_W:VT_ROWS, rows] = extra
            return carry
        lax.fori_loop(0, nt, prep_kv, 0)

    for hh, cols in heads:
        q = _rope(q_ref[:, cols].astype(F32), cq_ref[...], sq_ref[...], first_half) * (
            64 ** -0.5 * math.log2(math.e))
        qt_ref[hh] = jnp.concatenate([jnp.where(lane < 64, q, 0.0).T, jnp.where(lane >= 64, q, 0.0).T],
                                     axis=1).astype(BF16)
    n_kv = jnp.where(qi + q_off == 0, 1, nt)
    acc_ref[...] = jnp.zeros_like(acc_ref)

    def tile_rows(j):
        return pl.ds(pl.multiple_of(j * TM, TM), TM)

    def scores(j, st_ref):
        for hh, _ in heads:
            st_ref[hh] = _dot(kr_ref[hh, tile_rows(j), :], qt_ref[hh])

    def softmax_pv(j, st_ref, ms):
        out = []
        for hh, _ in heads:
            st = st_ref[hh]
            m_new = jnp.maximum(ms[hh], jnp.max(st, axis=0, keepdims=True))
            alpha = jnp.exp2(ms[hh] - m_new)
            p = jnp.exp2(st - m_new).astype(BF16)
            acc_ref[hh] = alpha * acc_ref[hh] + _dot(vt_ref[hh, :, tile_rows(j)], p)
            out.append(m_new)
        return tuple(out)

    def pair(jj, ms):
        scores(2 * jj + 1, st1_ref)
        ms = softmax_pv(2 * jj, st0_ref, ms)
        scores(2 * jj + 2, st0_ref)
        return softmax_pv(2 * jj + 1, st1_ref, ms)

    scores(0, st0_ref)
    m0 = jnp.full((1, 2 * TM), -1e30, F32)
    ms = lax.fori_loop(0, (n_kv - 1) // 2, pair, (m0,) * ATTN_HP)
    softmax_pv(n_kv - 1, st0_ref, ms)
    lv = lv_ref[...]
    lam = (jnp.exp(jnp.sum(lv[0:1, :] * lv[1:2, :], axis=-1, keepdims=True))
           - jnp.exp(jnp.sum(lv[2:3, :] * lv[3:4, :], axis=-1, keepdims=True)) + lam_init)
    for hh, cols in heads:
        on = acc_ref[hh, 0:HEAD_W, :] * (1.0 / acc_ref[hh, HEAD_W:HEAD_W + 1, :])
        o = (on[:, :TM] - lam * on[:, TM:]).T
        ms_o = jnp.mean(o * o, axis=-1, keepdims=True)
        o_ref[:, cols] = (o * lax.rsqrt(ms_o + 1e-5) * nw_ref[...] * (1.0 - lam_init)).astype(o_ref.dtype)


def _attn_call(qkv, cos_t, sin_t, lam_vecs, sub_norm, lam_init, q_off):
    B, T, _ = qkv.shape
    nt = T // TM
    nq = nt - q_off
    pw = ATTN_HP * HEAD_W
    npair = N_HEAD // ATTN_HP
    kern = functools.partial(_attn_kernel, lam_init=lam_init, q_off=q_off, nt=nt)
    return pl.pallas_call(
        kern,
        grid=(B, npair, nq),
        in_specs=[
            pl.BlockSpec((None, TM, pw), lambda b, h, i: (b, i + q_off, h)),
            pl.BlockSpec((None, T, pw), lambda b, h, i: (b, 0, npair + h)),
            pl.BlockSpec((None, T, pw), lambda b, h, i: (b, 0, 2 * npair + h)),
            pl.BlockSpec((TM, HEAD_W), lambda b, h, i: (i + q_off, 0)),
            pl.BlockSpec((TM, HEAD_W), lambda b, h, i: (i + q_off, 0)),
            pl.BlockSpec((T, HEAD_W), lambda b, h, i: (0, 0)),
            pl.BlockSpec((T, HEAD_W), lambda b, h, i: (0, 0)),
            pl.BlockSpec((4, 64), lambda b, h, i: (0, 0)),
            pl.BlockSpec((1, HEAD_W), lambda b, h, i: (0, 0)),
        ],
        out_specs=pl.BlockSpec((None, TM, pw), lambda b, h, i: (b, i + q_off, h)),
        out_shape=jax.ShapeDtypeStruct((B, T, MIX_W), BF16),
        scratch_shapes=[pltpu.VMEM((ATTN_HP, T, HEAD_W), BF16), pltpu.VMEM((ATTN_HP, VT_ROWS, T), BF16),
                        pltpu.VMEM((ATTN_HP, HEAD_W, 2 * TM), BF16),
                        pltpu.VMEM((ATTN_HP, TM, 2 * TM), F32), pltpu.VMEM((ATTN_HP, TM, 2 * TM), F32),
                        pltpu.VMEM((ATTN_HP, VT_ROWS, 2 * TM), F32)],
        compiler_params=_cparams(("arbitrary", "arbitrary", "arbitrary")),
        name="diff_attn",
    )(qkv, qkv, qkv, cos_t, sin_t, cos_t, sin_t, lam_vecs, sub_norm)


def _merge_kernel(h_ref, mod_ref, dnf_ref, dnb_ref, z_ref, yb_ref, yc_ref, g_ref, dnw_ref, wb_ref, wo_ref, o_ref):
    dnw = dnw_ref[...]
    ya_parts = []
    for hh in range(N_HEAD):
        cols = slice(hh * HEAD_W, (hh + 1) * HEAD_W)
        o = dnf_ref[:, cols] + dnb_ref[:, cols]
        ms = jnp.mean(o * o, axis=-1, keepdims=True)
        z = z_ref[:, cols].astype(F32)
        ya_parts.append(((o * lax.rsqrt(ms + 1e-6) * dnw) * (z * _sigmoid(z))).astype(BF16))
    ys = (jnp.concatenate(ya_parts, axis=1), yb_ref[...], yc_ref[...])
    mix = None
    for n in range(3):
        up = _dot(ys[n], wb_ref[n])
        term = _sigmoid(g_ref[:, n * D_MODEL:(n + 1) * D_MODEL].astype(F32)) * up
        mix = term if mix is None else mix + term
    y = _dot(mix.astype(BF16), wo_ref[...])
    o_ref[...] = h_ref[...] + mod_ref[2:3, :] * y


def _merge_call(h, mod, dnf, dnb, z, yb, yc, gates, dn_norm, wb, wo, t_off):
    B, T, _ = h.shape
    nt = T // TM - t_off
    row = lambda w: pl.BlockSpec((None, TM, w), lambda b, i: (b, i + t_off, 0))
    return pl.pallas_call(
        _merge_kernel,
        grid=(B, nt),
        in_specs=[
            row(D_MODEL),
            pl.BlockSpec((None, None, 6, D_MODEL), lambda b, i: _mod_index(b, i + t_off)),
            row(MIX_W), row(MIX_W), row(MIX_W), row(MIX_W), row(MIX_W), row(3 * D_MODEL),
            pl.BlockSpec((1, HEAD_W), lambda b, i: (0, 0)),
            pl.BlockSpec((3, MIX_W, D_MODEL), lambda b, i: (0, 0, 0)),
            pl.BlockSpec((D_MODEL, D_MODEL), lambda b, i: (0, 0)),
        ],
        out_specs=pl.BlockSpec((None, TM, D_MODEL), lambda b, i: (b, i, 0)),
        out_shape=jax.ShapeDtypeStruct((B, nt * TM, D_MODEL), F32),
        compiler_params=_cparams(("arbitrary", "arbitrary")),
        name="merge",
    )(h, mod, dnf, dnb, z, yb, yc, gates, dn_norm, wb, wo)


def _ffn_kernel(h_ref, mod_ref, nw_ref, wg_ref, wu_ref, wd_ref, fw_ref, o_ref, *, final):
    h = h_ref[...]
    x = _norm_modulate(h, nw_ref[...], mod_ref[3:4, :], mod_ref[4:5, :]).astype(BF16)
    acc = None
    for c in range(0, D_FF, FF_CHUNK):
        g = _dot(x, wg_ref[:, c:c + FF_CHUNK])
        u = _dot(x, wu_ref[:, c:c + FF_CHUNK])
        part = _dot(((g * _sigmoid(g)) * u).astype(BF16), wd_ref[c:c + FF_CHUNK, :])
        acc = part if acc is None else acc + part
    out = h + mod_ref[5:6, :] * acc
    if final:
        ms = jnp.mean(out * out, axis=-1, keepdims=True)
        out = out * lax.rsqrt(ms + 1e-6) * fw_ref[...]
    o_ref[...] = out


def _ffn_call(h, mod, nw, wg, wu, wd, fw, seg_off, final):
    B, T, _ = h.shape
    nt = T // TM
    return pl.pallas_call(
        functools.partial(_ffn_kernel, final=final),
        grid=(B, nt),
        in_specs=[
            pl.BlockSpec((None, TM, D_MODEL), lambda b, i: (b, i, 0)),
            pl.BlockSpec((None, None, 6, D_MODEL), lambda b, i: _mod_index(b, i + seg_off)),
            pl.BlockSpec((1, D_MODEL), lambda b, i: (0, 0)),
            pl.BlockSpec((D_MODEL, D_FF), lambda b, i: (0, 0)),
            pl.BlockSpec((D_MODEL, D_FF), lambda b, i: (0, 0)),
            pl.BlockSpec((D_FF, D_MODEL), lambda b, i: (0, 0)),
            pl.BlockSpec((1, D_MODEL), lambda b, i: (0, 0)),
        ],
        out_specs=pl.BlockSpec((None, TM, D_MODEL), lambda b, i: (b, i, 0)),
        out_shape=jax.ShapeDtypeStruct((B, T, D_MODEL), F32),
        compiler_params=_cparams(("arbitrary", "arbitrary")),
        name="ffn",
    )(h, mod, nw, wg, wu, wd, fw)


def _reorder_w_in(w):
    parts = [w[:, 0:1536], w[:, 1536:2048], w[:, 2064:2576], w[:, 2576:3088], w[:, 3088:4624], w[:, 4624:7696],
             w[:, 2048:2064], jnp.zeros((w.shape[0], IN_COLS_PAD - 7696), w.dtype)]
    return jnp.concatenate(parts, axis=1).astype(BF16)


def _block_diag(w):
    g, n, _ = w.shape
    return jnp.einsum('gij,gh->gihj', w, jnp.eye(g, dtype=w.dtype)).reshape(g * n, g * n)


def _lru_gate_weights(wa, ba, wi, bi):
    half = LRU_G // 2
    ws, bs = [], []
    for p in range(2):
        grp = slice(p * half, (p + 1) * half)
        ch = slice(p * MIX_W // 2, (p + 1) * MIX_W // 2)
        cols, bcols = [], []
        for d in range(2):
            cols += [_block_diag(wa[d, grp]), _block_diag(wi[d, grp])]
            bcols += [ba[d, ch], bi[d, ch]]
        ws.append(jnp.concatenate(cols, axis=1))
        bs.append(jnp.concatenate(bcols)[None, :])
    return jnp.stack(ws).astype(BF16), jnp.stack(bs)


def _rope_tables(seq):
    rows = seq // GRID_W
    rowp = jnp.repeat(jnp.arange(rows, dtype=F32), GRID_W)
    colp = jnp.tile(jnp.arange(GRID_W, dtype=F32), rows)
    inv = ROPE_BASE ** (-jnp.arange(ROPE_NF, dtype=F32) / ROPE_NF)
    ar, ac = rowp[:, None] * inv, colp[:, None] * inv
    cs = jnp.concatenate([jnp.cos(ar), jnp.cos(ar), jnp.cos(ac), jnp.cos(ac)], axis=1)
    sn = jnp.concatenate([-jnp.sin(ar), jnp.sin(ar), -jnp.sin(ac), jnp.sin(ac)], axis=1)
    cs = jnp.concatenate([jnp.ones((CTX_LEN, 64), F32), cs], axis=0)
    sn = jnp.concatenate([jnp.zeros((CTX_LEN, 64), F32), sn], axis=0)
    return jnp.tile(cs, (1, 2)), jnp.tile(sn, (1, 2))


def kernel(x, c, ctx, c_ctx, w_mod, b_mod, norm_mix, norm_ffn, w_in, dn_conv, dn_a_log, dn_dt_bias, dn_norm, lru_conv_w, lru_conv_b, lru_wa, lru_ba, lru_wi, lru_bi, lru_lambda, da_lambda, da_norm, w_branch, w_out, w_ffn_gate, w_ffn_up, w_ffn_down, norm_final):
    B, S, D = x.shape
    depth = w_mod.shape[0]
    rows = jnp.concatenate([c, c_ctx[None, :], jnp.zeros((16 - B - 1, D), F32)], axis=0)
    mod = _mod_call(rows, w_mod, b_mod).reshape(depth, 16, 6, D)
    cos_t, sin_t = _rope_tables(S)
    h = jnp.concatenate([ctx, x], axis=1)
    for l in range(depth):
        last = l == depth - 1
        t_off = 1 if last else 0
        lam_init = 0.8 - 0.6 * math.exp(-0.3 * l)
        mod_l = jnp.stack([jnp.broadcast_to(mod[l, B][None], (B, 6, D)), mod[l, :B]], axis=1)
        dn_qkv, dn_z, lru_x, lru_y, da_qkv, gates, ab = _inproj_call(
            h, mod_l, norm_mix[l][None, :], _reorder_w_in(w_in[l]))
        dn_f, dn_b = _dn_seq_call(_dn_chunk_call(dn_qkv, ab, dn_conv[l], dn_a_log[l], dn_dt_bias[l]))
        wg, bg = _lru_gate_weights(lru_wa[l], lru_ba[l], lru_wi[l], lru_bi[l])
        yb = _lru_call(lru_x, lru_y, lru_conv_w[l], lru_conv_b[l][None, :], wg, bg, lru_lambda[l])
        yc = _attn_call(da_qkv, cos_t, sin_t, da_lambda[l], da_norm[l][None, :], lam_init, t_off)
        h = _merge_call(h, mod_l, dn_f, dn_b, dn_z, yb, yc, gates, dn_norm[l][None, :],
                        w_branch[l].astype(BF16), w_out[l].astype(BF16), t_off)
        h = _ffn_call(h, mod_l, norm_ffn[l][None, :], w_ffn_gate[l].astype(BF16), w_ffn_up[l].astype(BF16),
                      w_ffn_down[l].astype(BF16), norm_final[None, :], t_off, last)
    return h
```

```python
import functools
import math

import jax
import jax.numpy as jnp
import numpy as np
from jax import lax
from jax.experimental import pallas as pl
from jax.experimental.pallas import tpu as pltpu

F32 = jnp.float32
BF16 = jnp.bfloat16

D_MODEL = 1024
CTX_LEN = 256
GRID_W = 64
MIX_W = 512
N_HEAD = 4
HEAD_W = 128
DN_CHUNK = 64
LRU_C = 8.0
LRU_G = 8
ROPE_BASE = 10000.0
ROPE_NF = 16
D_FF = 2816
TM = 256
FF_CHUNK = 256
VT_ROWS = HEAD_W + 16
ATTN_HP = 2
DN_SEQ_BB = 2
VMEM_LIMIT = 56 * 1024 * 1024

C_DNQKV, C_DNZ, C_LRUX, C_LRUY, C_DAQKV, C_GATE, C_AB = 0, 1536, 2048, 2560, 3072, 4608, 7680
IN_COLS_PAD = 7808


def _cparams(sem):
    return pltpu.CompilerParams(dimension_semantics=sem, vmem_limit_bytes=VMEM_LIMIT)


def _sigmoid(x):
    return 1.0 / (1.0 + jnp.exp(-x))


def _softplus(x):
    return jnp.maximum(x, 0.0) + jnp.log(1.0 + jnp.exp(-jnp.abs(x)))


def _dot(a, b):
    return jnp.dot(a, b, preferred_element_type=F32)


def _dot_nt(a, b):
    return lax.dot_general(a, b, (((1,), (1,)), ((), ())), preferred_element_type=F32)


def _dot_tn(a, b):
    return lax.dot_general(a, b, (((0,), (0,)), ((), ())), preferred_element_type=F32)


def _mod_kernel(s_ref, w_ref, b_ref, o_ref):
    s = s_ref[...]
    s = s * _sigmoid(s)
    o_ref[...] = jnp.dot(s, w_ref[...], preferred_element_type=F32,
                         precision=lax.Precision.HIGHEST) + b_ref[...]


def _mod_call(rows, w_mod, b_mod):
    depth = w_mod.shape[0]
    n6 = w_mod.shape[2]
    tn = 1024
    return pl.pallas_call(
        _mod_kernel,
        grid=(depth, n6 // tn),
        in_specs=[
            pl.BlockSpec((16, D_MODEL), lambda l, j: (0, 0)),
            pl.BlockSpec((None, D_MODEL, tn), lambda l, j: (l, 0, j)),
            pl.BlockSpec((None, 1, tn), lambda l, j: (l, 0, j)),
        ],
        out_specs=pl.BlockSpec((None, 16, tn), lambda l, j: (l, 0, j)),
        out_shape=jax.ShapeDtypeStruct((depth, 16, n6), F32),
        compiler_params=_cparams(("arbitrary", "arbitrary")),
        name="adaln_mod",
    )(rows, w_mod, b_mod.reshape(depth, 1, n6))


def _mod_index(b, i):
    return (b, jnp.minimum(i, 1), 0, 0)


def _norm_modulate(x, nw, shift, scale, eps=1e-6):
    ms = jnp.mean(x * x, axis=-1, keepdims=True)
    y = x * lax.rsqrt(ms + eps) * nw
    return y * (1.0 + scale) + shift


_INPROJ_SEGS = (
    (C_DNQKV, 1536), (C_DNZ, 512), (C_LRUX, 512), (C_LRUY, 512), (C_DAQKV, 1536), (C_GATE, 3072), (C_AB, 128))


def _inproj_kernel(h_ref, mod_ref, nw_ref, w_ref, *out_refs):
    u = _norm_modulate(h_ref[...], nw_ref[...], mod_ref[0:1, :], mod_ref[1:2, :]).astype(BF16)
    for (c0, width), o_ref in zip(_INPROJ_SEGS, out_refs):
        step = min(width, 512)
        for c in range(0, width, step):
            o_ref[:, c:c + step] = _dot(u, w_ref[:, c0 + c:c0 + c + step]).astype(o_ref.dtype)


def _inproj_call(h, mod, nw, w):
    B, T, _ = h.shape
    nt = T // TM
    dts = (BF16, BF16, F32, BF16, BF16, BF16, F32)
    return pl.pallas_call(
        _inproj_kernel,
        grid=(B, nt),
        in_specs=[
            pl.BlockSpec((None, TM, D_MODEL), lambda b, i: (b, i, 0)),
            pl.BlockSpec((None, None, 6, D_MODEL), _mod_index),
            pl.BlockSpec((1, D_MODEL), lambda b, i: (0, 0)),
            pl.BlockSpec((D_MODEL, IN_COLS_PAD), lambda b, i: (0, 0)),
        ],
        out_specs=[pl.BlockSpec((None, TM, wd), lambda b, i: (b, i, 0)) for _, wd in _INPROJ_SEGS],
        out_shape=[jax.ShapeDtypeStruct((B, T, wd), dt) for (_, wd), dt in zip(_INPROJ_SEGS, dts)],
        compiler_params=_cparams(("arbitrary", "arbitrary")),
        name="in_proj",
    )(h, mod, nw, w)


def _dwconv_tile(x, prev_last, next0, next1, w, row):
    R = x.shape[0]
    xm1 = jnp.where(row == 0, prev_last, pltpu.roll(x, 1, 0))
    xp1 = jnp.where(row == R - 1, next0, pltpu.roll(x, R - 1, 0))
    xp2 = jnp.where(row == R - 2, next0, jnp.where(row == R - 1, next1, pltpu.roll(x, R - 2, 0)))
    return w[0:1, :] * xm1 + w[1:2, :] * x + w[2:3, :] * xp1 + w[3:4, :] * xp2


def _halo_valid(i, nt):
    vp = jnp.where(i >= 2, 1.0, 0.0).astype(F32)
    vn = jnp.where(jnp.logical_and(i >= 1, i <= nt - 2), 1.0, 0.0).astype(F32)
    return vp, vn


def _dn_constants():
    r = np.arange(TM)[:, None]
    c = np.arange(TM)[None, :]
    same = (r // DN_CHUNK) == (c // DN_CHUNK)
    shifts = np.stack([c == r - 1, c == r + 1, c == r + 2]).astype(np.float32)
    sums = np.concatenate([same & (c <= r), same], axis=0).astype(np.float32)
    masks = np.stack([same & (c <= r), same & (c >= r), same & (c < r), same & (c > r),
                      (r // 16) == (c // 16), r == c]).astype(np.float32)
    return jnp.asarray(shifts, BF16), jnp.asarray(sums, BF16), jnp.asarray(masks, F32)


def _dn_chunk_kernel(x_ref, xp_ref, xn_ref, ab_ref, cw_ref, al_ref, dtb_ref, sh_ref, sm_ref, mk_ref,
                     wf_ref, uf_ref, qf_ref, kf_ref, qkf_ref, wb_ref, ub_ref, qb_ref, kb_ref, qkb_ref, ge_ref,
                     s_ref, gc_ref, bt_ref, gct_ref, gt_ref, *, nt):
    i = pl.program_id(1)
    vp, vn = _halo_valid(i, nt)
    row8 = lax.broadcasted_iota(jnp.int32, (8, HEAD_W), 0)
    for cb2 in range(3 * N_HEAD // 2):
        c2 = slice(cb2 * 2 * HEAD_W, (cb2 + 1) * 2 * HEAD_W)
        xb = x_ref[:, c2]
        xm1, xp1, xp2 = _dot(sh_ref[0], xb), _dot(sh_ref[1], xb), _dot(sh_ref[2], xb)
        xf = xb.astype(F32)
        for half in range(2):
            cb = cb2 * 2 + half
            cols = slice(cb * HEAD_W, (cb + 1) * HEAD_W)
            hc = slice(half * HEAD_W, (half + 1) * HEAD_W)
            w = cw_ref[:, cols]
            y = w[0:1, :] * xm1[:, hc] + w[1:2, :] * xf[:, hc] + w[2:3, :] * xp1[:, hc] + w[3:4, :] * xp2[:, hc]
            prev = xp_ref[:, cols].astype(F32)[15:16, :] * vp
            nxt = xn_ref[:, cols].astype(F32)
            n0, n1 = nxt[0:1, :] * vn, nxt[1:2, :] * vn
            top = jnp.where(row8 == 0, w[0:1, :] * prev, 0.0)
            bot = jnp.where(row8 == 6, w[3:4, :] * n0, jnp.where(row8 == 7, w[2:3, :] * n0 + w[3:4, :] * n1, 0.0))
            y = jnp.concatenate([y[0:8, :] + top, y[8:TM - 8, :], y[TM - 8:TM, :] + bot], axis=0)
            s = y * _sigmoid(y)
            if cb < 2 * N_HEAD:
                ss = jnp.sum(s * s, axis=-1, keepdims=True)
                s = s * (lax.rsqrt(ss + 1e-6) * (HEAD_W ** -0.5 if cb < N_HEAD else 1.0))
            s_ref[cb] = s

    ab = ab_ref[...]
    lane = lax.broadcasted_iota(jnp.int32, (TM, 128), 1)
    g = -jnp.exp(al_ref[...]) * _softplus(ab + dtb_ref[...])
    g1 = g.astype(BF16)
    r1 = g - g1.astype(F32)
    g2 = r1.astype(BF16)
    g3 = (r1 - g2.astype(F32)).astype(BF16)
    sums = _dot(sm_ref[...], jnp.concatenate([g1, g2, g3], axis=1))
    sums = sums[:, 0:128] + sums[:, 128:256] + sums[:, 256:384]
    g_f = sums[0:TM, :]
    gtot = sums[TM:2 * TM, :]
    gc = jnp.where(lane < N_HEAD, g_f, gtot - g_f + g)
    gc_ref[...] = gc
    bt_ref[...] = _sigmoid(ab)
    gt_ref[...] = gtot
    gct_ref[...] = gc.T
    ge_ref[...] = jnp.exp(gtot)
    out_refs = ((wf_ref, uf_ref, qf_ref, kf_ref, qkf_ref), (wb_ref, ub_ref, qb_ref, kb_ref, qkb_ref))
    sub8 = lax.broadcasted_iota(jnp.int32, (8, TM), 0)

    probs = [(h, d) for h in range(N_HEAD) for d in range(2)]
    kk, qk0 = {}, {}
    for h in range(N_HEAD):
        kb16 = s_ref[N_HEAD + h].astype(BF16)
        kk[h] = _dot_nt(kb16, kb16)
        qk0[h] = _dot_nt(s_ref[h].astype(BF16), kb16)
    tmat, dblk, lo_b, rhs_b = {}, {}, {}, {}
    for h, d in probs:
        q = s_ref[h]
        k = s_ref[N_HEAD + h]
        v = s_ref[2 * N_HEAD + h]
        idx = d * N_HEAD + h
        sel = lane == idx
        gc_col = jnp.sum(jnp.where(sel, gc_ref[...], 0.0), axis=1, keepdims=True)
        gt_col = jnp.sum(jnp.where(sel, gt_ref[...], 0.0), axis=1, keepdims=True)
        bt_col = jnp.sum(jnp.where(lane == idx + 2 * N_HEAD, bt_ref[...], 0.0), axis=1, keepdims=True)
        gc_row = jnp.sum(jnp.where(sub8 == idx, gct_ref[0:8, :], 0.0), axis=0, keepdims=True)
        e = jnp.exp(jnp.minimum(gc_col - gc_row, 0.0))
        a = kk[h] * (e * mk_ref[2 + d] * bt_col)
        eg = jnp.exp(gc_col)
        _, _, q_r, k_r, qk_r = out_refs[d]
        q_r[h] = (q * eg).astype(BF16)
        k_r[h] = (k * jnp.exp(gt_col - gc_col)).astype(BF16)
        qk_r[h] = (qk0[h] * (e * mk_ref[d])).astype(BF16)
        dm = a * mk_ref[4]
        tmat[h, d] = mk_ref[5] - dm
        lo_b[h, d] = (a - dm).astype(BF16)
        rhs_b[h, d] = jnp.concatenate([k * (bt_col * eg), v * bt_col], axis=1).astype(BF16)
        dblk[h, d] = dm.astype(BF16)
    pw = {pk: _dot(dblk[pk], dblk[pk]) for pk in probs}
    for m in range(3):
        for pk in probs:
            pb = pw[pk].astype(BF16)
            tmat[pk] = tmat[pk] + _dot(tmat[pk].astype(BF16), pb)
            if m < 2:
                pw[pk] = _dot(pb, pb)
    x0, mt, m2 = {}, {}, {}
    for pk in probs:
        tb = tmat[pk].astype(BF16)
        x0[pk] = _dot(tb, rhs_b[pk])
        mt[pk] = _dot(tb, lo_b[pk]).astype(BF16)
    for pk in probs:
        m2[pk] = _dot(mt[pk], mt[pk]).astype(BF16)
    ys = {pk: x0[pk] + _dot(m2[pk], x0[pk].astype(BF16)) for pk in probs}
    for h, d in probs:
        x = ys[h, d] - _dot(mt[h, d], ys[h, d].astype(BF16))
        w_r, u_r = out_refs[d][0], out_refs[d][1]
        w_r[h] = x[:, :HEAD_W].astype(BF16)
        u_r[h] = x[:, HEAD_W:]


def _dn_chunk_call(qkv, ab, conv_w, a_log, dt_bias):
    B, T, W = qkv.shape
    nt = T // TM
    hb = TM // 16
    last16 = T // 16 - 1
    al = jnp.zeros((1, 128), F32).at[0, :2 * N_HEAD].set(a_log.reshape(-1))
    dtb = jnp.zeros((1, 128), F32).at[0, :2 * N_HEAD].set(dt_bias.reshape(-1))
    shifts, sums, masks = _dn_constants()
    hspec = lambda w: pl.BlockSpec((None, N_HEAD, TM, w), lambda b, i: (b, 0, i, 0))
    hshape = lambda w, dt: jax.ShapeDtypeStruct((B, N_HEAD, T, w), dt)
    const = lambda a: pl.BlockSpec(a.shape, lambda b, i: (0,) * a.ndim)
    per_dir_specs = [hspec(HEAD_W), hspec(HEAD_W), hspec(HEAD_W), hspec(HEAD_W), hspec(TM)]
    per_dir_shapes = [hshape(HEAD_W, BF16), hshape(HEAD_W, F32), hshape(HEAD_W, BF16), hshape(HEAD_W, BF16),
                      hshape(TM, BF16)]
    return pl.pallas_call(
        functools.partial(_dn_chunk_kernel, nt=nt),
        grid=(B, nt),
        in_specs=[
            pl.BlockSpec((None, TM, W), lambda b, i: (b, i, 0)),
            pl.BlockSpec((None, 16, W), lambda b, i: (b, jnp.maximum(i * hb - 1, 0), 0)),
            pl.BlockSpec((None, 16, W), lambda b, i: (b, jnp.minimum((i + 1) * hb, last16), 0)),
            pl.BlockSpec((None, TM, 128), lambda b, i: (b, i, 0)),
            pl.BlockSpec((4, W), lambda b, i: (0, 0)),
            pl.BlockSpec((1, 128), lambda b, i: (0, 0)),
            pl.BlockSpec((1, 128), lambda b, i: (0, 0)),
            const(shifts), const(sums), const(masks),
        ],
        out_specs=per_dir_specs + per_dir_specs + [pl.BlockSpec((None, TM, 128), lambda b, i: (b, i, 0))],
        out_shape=per_dir_shapes + per_dir_shapes + [jax.ShapeDtypeStruct((B, T, 128), F32)],
        scratch_shapes=[
            pltpu.VMEM((3 * N_HEAD, TM, HEAD_W), F32),
            pltpu.VMEM((TM, 128), F32), pltpu.VMEM((TM, 128), F32),
            pltpu.VMEM((128, TM), F32), pltpu.VMEM((TM, 128), F32),
        ],
        compiler_params=_cparams(("arbitrary", "arbitrary")),
        name="dn_chunk",
    )(qkv, qkv, qkv, ab, conv_w, al, dtb, shifts, sums, masks)


def _dn_seq_kernel(wf_ref, uf_ref, qf_ref, kf_ref, qkf_ref, gef_ref, wb_ref, ub_ref, qb_ref, kb_ref, qkb_ref,
                   geb_ref, of_ref, ob_ref, st_ref):
    C = DN_CHUNK
    nc = TM // C

    @pl.when(pl.program_id(1) == 0)
    def _():
        st_ref[...] = jnp.zeros_like(st_ref)

    ins = ((wf_ref, uf_ref, qf_ref, kf_ref, qkf_ref, gef_ref, of_ref),
           (wb_ref, ub_ref, qb_ref, kb_ref, qkb_ref, geb_ref, ob_ref))
    chains = [(d, bb, h) for d in range(2) for bb in range(DN_SEQ_BB) for h in range(N_HEAD)]
    for ci in range(nc):
        st, sb, ws, qs, vb = {}, {}, {}, {}, {}
        for ch in chains:
            d, bb, h = ch
            c = ci if d == 0 else nc - 1 - ci
            rows = slice(c * C, (c + 1) * C)
            st[ch] = st_ref[(d * DN_SEQ_BB + bb) * N_HEAD + h]
            sb[ch] = st[ch].astype(BF16)
            ws[ch] = _dot(ins[d][0][bb, h, rows, :], sb[ch])
            qs[ch] = _dot(ins[d][2][bb, h, rows, :], sb[ch])
        for ch in chains:
            d, bb, h = ch
            c = ci if d == 0 else nc - 1 - ci
            rows = slice(c * C, (c + 1) * C)
            vb[ch] = (ins[d][1][bb, h, rows, :] - ws[ch]).astype(BF16)
        for ch in chains:
            d, bb, h = ch
            c = ci if d == 0 else nc - 1 - ci
            rows = slice(c * C, (c + 1) * C)
            _, _, _, k_r, qk_r, ge_r, o_r = ins[d]
            si = d * N_HEAD + h
            ge = ge_r[bb, c * C:c * C + 8, :][0:1, si:si + 1]
            o_r[bb, rows, h * HEAD_W:(h + 1) * HEAD_W] = qs[ch] + _dot(qk_r[bb, h, rows, c * C:(c + 1) * C], vb[ch])
            st_ref[(d * DN_SEQ_BB + bb) * N_HEAD + h] = ge * st[ch] + _dot_tn(k_r[bb, h, rows, :], vb[ch])


def _dn_seq_call(chunk_outs):
    wf, uf, qf, kf, qkf, wb, ub, qb, kb, qkb, ge = chunk_outs
    B, _, T, _ = wf.shape
    nt = T // TM
    bb = DN_SEQ_BB
    tf = lambda s: s
    tb = lambda s: jnp.where(s == 0, 0, nt - s)
    hspec = lambda w, t: pl.BlockSpec((bb, N_HEAD, TM, w), lambda b, s: (b, 0, t(s), 0))
    gspec = lambda t: pl.BlockSpec((bb, TM, 128), lambda b, s: (b, t(s), 0))
    dir_specs = lambda t: [hspec(HEAD_W, t), hspec(HEAD_W, t), hspec(HEAD_W, t), hspec(HEAD_W, t), hspec(TM, t),
                           gspec(t)]
    ospec = lambda t: pl.BlockSpec((bb, TM, MIX_W), lambda b, s: (b, t(s), 0))
    return pl.pallas_call(
        _dn_seq_kernel,
        grid=(B // bb, nt),
        in_specs=dir_specs(tf) + dir_specs(tb),
        out_specs=[ospec(tf), ospec(tb)],
        out_shape=[jax.ShapeDtypeStruct((B, T, MIX_W), F32)] * 2,
        scratch_shapes=[pltpu.VMEM((2 * bb * N_HEAD, HEAD_W, HEAD_W), F32)],
        compiler_params=_cparams(("arbitrary", "arbitrary")),
        name="dn_seq",
    )(wf, uf, qf, kf, qkf, ge, wb, ub, qb, kb, qkb, ge)


def _gelu_tanh(x):
    return 0.5 * x * (1.0 + jnp.tanh(math.sqrt(2.0 / math.pi) * (x + 0.044715 * (x * x * x))))


def _lru_kernel(x_ref, y_ref, cw_ref, cb_ref, wg_ref, bg_ref, lam_ref, o_ref, hf_ref, *, nt):
    CW = x_ref.shape[1]
    T = x_ref.shape[0]
    G = TM // 8
    row = lax.broadcasted_iota(jnp.int32, (TM, CW), 0)
    sub = row % 8

    def coeffs(t, d):
        base = pl.multiple_of(t * TM, TM)
        x = x_ref[pl.ds(base, TM), :]
        xp = x_ref[pl.ds(pl.multiple_of(jnp.maximum(base - 8, 0), 8), 8), :]
        xn = x_ref[pl.ds(pl.multiple_of(jnp.minimum(base + TM, T - 8), 8), 8), :]
        vp, vn = _halo_valid(t, nt)
        xc = _dwconv_tile(x, xp[7:8, :] * vp, xn[0:1, :] * vn, xn[1:2, :] * vn, cw_ref[...], row) + cb_ref[...]
        z = _dot(xc.astype(BF16), wg_ref[:, d * 2 * CW:(d + 1) * 2 * CW]) + bg_ref[:, d * 2 * CW:(d + 1) * 2 * CW]
        rg = _sigmoid(z[:, :CW])
        ig = _sigmoid(z[:, CW:])
        log_a = (-LRU_C * _softplus(-lam_ref[d:d + 1, :])) * rg
        a = jnp.exp(log_a)
        b = jnp.sqrt(1.0 - a * a) * (ig * xc)
        return a, b

    def scan_tile(a, b, h_in, d):
        for s in (1, 2, 4):
            if d == 0:
                a_s, b_s, valid = pltpu.roll(a, s, 0), pltpu.roll(b, s, 0), sub >= s
            else:
                a_s, b_s, valid = pltpu.roll(a, TM - s, 0), pltpu.roll(b, TM - s, 0), sub < 8 - s
            b = jnp.where(valid, a * b_s + b, b)
            a = jnp.where(valid, a * a_s, a)
        hs = [None] * G
        h = h_in
        for g in (range(G) if d == 0 else range(G - 1, -1, -1)):
            hg = b[g * 8:(g + 1) * 8, :] + a[g * 8:(g + 1) * 8, :] * h
            hs[g] = hg
            h = hg[7:8, :] if d == 0 else hg[0:1, :]
        return jnp.concatenate(hs, axis=0), h

    def fwd(t, h):
        a, b = coeffs(t, 0)
        hf, h = scan_tile(a, b, h, 0)
        hf_ref[pl.ds(pl.multiple_of(t * TM, TM), TM), :] = hf
        return h

    lax.fori_loop(0, nt, fwd, jnp.zeros((1, CW), F32))

    def bwd(s, h):
        t = jnp.where(s == 0, 0, nt - s)
        a, b = coeffs(t, 1)
        hb, h = scan_tile(a, b, h, 1)
        rows = pl.ds(pl.multiple_of(t * TM, TM), TM)
        y = y_ref[rows, :].astype(F32)
        o_ref[rows, :] = ((hf_ref[rows, :] + hb) * _gelu_tanh(y)).astype(o_ref.dtype)
        return h

    lax.fori_loop(0, nt, bwd, jnp.zeros((1, CW), F32))


def _lru_call(x, y, conv_w, conv_b, wg, bg, lam):
    B, T, W = x.shape
    cw = W // 2
    nt = T // TM
    return pl.pallas_call(
        functools.partial(_lru_kernel, nt=nt),
        grid=(B, 2),
        in_specs=[
            pl.BlockSpec((None, T, cw), lambda b, p: (b, 0, p)),
            pl.BlockSpec((None, T, cw), lambda b, p: (b, 0, p)),
            pl.BlockSpec((4, cw), lambda b, p: (0, p)),
            pl.BlockSpec((1, cw), lambda b, p: (0, p)),
            pl.BlockSpec((None, cw, 4 * cw), lambda b, p: (p, 0, 0)),
            pl.BlockSpec((None, 1, 4 * cw), lambda b, p: (p, 0, 0)),
            pl.BlockSpec((2, cw), lambda b, p: (0, p)),
        ],
        out_specs=pl.BlockSpec((None, T, cw), lambda b, p: (b, 0, p)),
        out_shape=jax.ShapeDtypeStruct((B, T, W), BF16),
        scratch_shapes=[pltpu.VMEM((T, cw), F32)],
        compiler_params=_cparams(("arbitrary", "arbitrary")),
        name="rglru",
    )(x, y, conv_w, conv_b, wg, bg, lam)


def _rope(x, cs, sn, first_half):
    swapped = jnp.where(first_half, pltpu.roll(x, 112, 1), pltpu.roll(x, 16, 1))
    return x * cs + swapped * sn


def _attn_kernel(q_ref, k_ref, v_ref, cq_ref, sq_ref, ck_ref, sk_ref, lv_ref, nw_ref, o_ref, kr_ref, vt_ref,
                 qt_ref, st0_ref, st1_ref, acc_ref, *, lam_init, q_off, nt):
    qi = pl.program_id(2)
    lane = lax.broadcasted_iota(jnp.int32, (TM, HEAD_W), 1)
    first_half = (lane % 32) < 16
    heads = [(hh, slice(hh * HEAD_W, (hh + 1) * HEAD_W)) for hh in range(ATTN_HP)]

    @pl.when(qi == 0)
    def _():
        extra = (lax.broadcasted_iota(jnp.int32, (VT_ROWS - HEAD_W, TM), 0) == 0).astype(BF16)

        def prep_kv(t, carry):
            rows = pl.ds(pl.multiple_of(t * TM, TM), TM)
            for hh, cols in heads:
                kr_ref[hh, rows, :] = _rope(k_ref[rows, cols].astype(F32), ck_ref[rows, :], sk_ref[rows, :],
                                            first_half).astype(BF16)
                vt_ref[hh, 0:HEAD_W, rows] = v_ref[rows, cols].astype(F32).T.astype(BF16)
                vt_ref[hh, HEAD_W:VT_ROWS, rows] = extra
            return carry
        lax.fori_loop(0, nt, prep_kv, 0)

    for hh, cols in heads:
        q = _rope(q_ref[:, cols].astype(F32), cq_ref[...], sq_ref[...], first_half) * (
            64 ** -0.5 * math.log2(math.e))
        qt_ref[hh] = jnp.concatenate([jnp.where(lane < 64, q, 0.0).T, jnp.where(lane >= 64, q, 0.0).T],
                                     axis=1).astype(BF16)
    n_kv = jnp.where(qi + q_off == 0, 1, nt)
    acc_ref[...] = jnp.zeros_like(acc_ref)

    def tile_rows(j):
        return pl.ds(pl.multiple_of(j * TM, TM), TM)

    def scores(j, st_ref):
        for hh, _ in heads:
            st_ref[hh] = _dot(kr_ref[hh, tile_rows(j), :], qt_ref[hh])

    def softmax_pv(j, st_ref, ms):
        sts = [st_ref[hh] for hh, _ in heads]
        m_new = [jnp.maximum(ms[hh], jnp.max(sts[hh], axis=0, keepdims=True)) for hh, _ in heads]
        ps = [jnp.exp2(sts[hh] - m_new[hh]).astype(BF16) for hh, _ in heads]
        for hh, _ in heads:
            acc_ref[hh] = (jnp.exp2(ms[hh] - m_new[hh]) * acc_ref[hh]
                           + _dot(vt_ref[hh, :, tile_rows(j)], ps[hh]))
        return tuple(m_new)

    def quad(jj, ms):
        for t in range(0, 4, 2):
            scores(4 * jj + t + 1, st1_ref)
            ms = softmax_pv(4 * jj + t, st0_ref, ms)
            scores(4 * jj + t + 2, st0_ref)
            ms = softmax_pv(4 * jj + t + 1, st1_ref, ms)
        return ms

    scores(0, st0_ref)
    m0 = jnp.full((1, 2 * TM), -1e30, F32)
    ms = lax.fori_loop(0, (n_kv - 1) // 4, quad, (m0,) * ATTN_HP)
    softmax_pv(n_kv - 1, st0_ref, ms)
    lv = lv_ref[...]
    lam = (jnp.exp(jnp.sum(lv[0:1, :] * lv[1:2, :], axis=-1, keepdims=True))
           - jnp.exp(jnp.sum(lv[2:3, :] * lv[3:4, :], axis=-1, keepdims=True)) + lam_init)
    for hh, cols in heads:
        on = acc_ref[hh, 0:HEAD_W, :] * (1.0 / acc_ref[hh, HEAD_W:HEAD_W + 1, :])
        o = (on[:, :TM] - lam * on[:, TM:]).T
        ms_o = jnp.mean(o * o, axis=-1, keepdims=True)
        o_ref[:, cols] = (o * lax.rsqrt(ms_o + 1e-5) * nw_ref[...] * (1.0 - lam_init)).astype(o_ref.dtype)


def _attn_call(qkv, cos_t, sin_t, lam_vecs, sub_norm, lam_init, q_off):
    B, T, _ = qkv.shape
    nt = T // TM
    nq = nt - q_off
    assert (nt - 1) % 4 == 0, "the key-tile loop advances four latent tiles per trip"
    pw = ATTN_HP * HEAD_W
    npair = N_HEAD // ATTN_HP
    kern = functools.partial(_attn_kernel, lam_init=lam_init, q_off=q_off, nt=nt)
    return pl.pallas_call(
        kern,
        grid=(B, npair, nq),
        in_specs=[
            pl.BlockSpec((None, TM, pw), lambda b, h, i: (b, i + q_off, h)),
            pl.BlockSpec((None, T, pw), lambda b, h, i: (b, 0, npair + h)),
            pl.BlockSpec((None, T, pw), lambda b, h, i: (b, 0, 2 * npair + h)),
            pl.BlockSpec((TM, HEAD_W), lambda b, h, i: (i + q_off, 0)),
            pl.BlockSpec((TM, HEAD_W), lambda b, h, i: (i + q_off, 0)),
            pl.BlockSpec((T, HEAD_W), lambda b, h, i: (0, 0)),
            pl.BlockSpec((T, HEAD_W), lambda b, h, i: (0, 0)),
            pl.BlockSpec((4, 64), lambda b, h, i: (0, 0)),
            pl.BlockSpec((1, HEAD_W), lambda b, h, i: (0, 0)),
        ],
        out_specs=pl.BlockSpec((None, TM, pw), lambda b, h, i: (b, i + q_off, h)),
        out_shape=jax.ShapeDtypeStruct((B, T, MIX_W), BF16),
        scratch_shapes=[pltpu.VMEM((ATTN_HP, T, HEAD_W), BF16), pltpu.VMEM((ATTN_HP, VT_ROWS, T), BF16),
                        pltpu.VMEM((ATTN_HP, HEAD_W, 2 * TM), BF16),
                        pltpu.VMEM((ATTN_HP, TM, 2 * TM), F32), pltpu.VMEM((ATTN_HP, TM, 2 * TM), F32),
                        pltpu.VMEM((ATTN_HP, VT_ROWS, 2 * TM), F32)],
        compiler_params=_cparams(("arbitrary", "arbitrary", "arbitrary")),
        name="diff_attn",
    )(qkv, qkv, qkv, cos_t, sin_t, cos_t, sin_t, lam_vecs, sub_norm)


def _merge_kernel(h_ref, mod_ref, dnf_ref, dnb_ref, z_ref, yb_ref, yc_ref, g_ref, dnw_ref, wb_ref, wo_ref, o_ref):
    dnw = dnw_ref[...]
    ya_parts = []
    for hh in range(N_HEAD):
        cols = slice(hh * HEAD_W, (hh + 1) * HEAD_W)
        o = dnf_ref[:, cols] + dnb_ref[:, cols]
        ms = jnp.mean(o * o, axis=-1, keepdims=True)
        z = z_ref[:, cols].astype(F32)
        ya_parts.append(((o * lax.rsqrt(ms + 1e-6) * dnw) * (z * _sigmoid(z))).astype(BF16))
    ys = (jnp.concatenate(ya_parts, axis=1), yb_ref[...], yc_ref[...])
    mix = None
    for n in range(3):
        up = _dot(ys[n], wb_ref[n])
        term = _sigmoid(g_ref[:, n * D_MODEL:(n + 1) * D_MODEL].astype(F32)) * up
        mix = term if mix is None else mix + term
    y = _dot(mix.astype(BF16), wo_ref[...])
    o_ref[...] = h_ref[...] + mod_ref[2:3, :] * y


def _merge_call(h, mod, dnf, dnb, z, yb, yc, gates, dn_norm, wb, wo, t_off):
    B, T, _ = h.shape
    nt = T // TM - t_off
    row = lambda w: pl.BlockSpec((None, TM, w), lambda b, i: (b, i + t_off, 0))
    return pl.pallas_call(
        _merge_kernel,
        grid=(B, nt),
        in_specs=[
            row(D_MODEL),
            pl.BlockSpec((None, None, 6, D_MODEL), lambda b, i: _mod_index(b, i + t_off)),
            row(MIX_W), row(MIX_W), row(MIX_W), row(MIX_W), row(MIX_W), row(3 * D_MODEL),
            pl.BlockSpec((1, HEAD_W), lambda b, i: (0, 0)),
            pl.BlockSpec((3, MIX_W, D_MODEL), lambda b, i: (0, 0, 0)),
            pl.BlockSpec((D_MODEL, D_MODEL), lambda b, i: (0, 0)),
        ],
        out_specs=pl.BlockSpec((None, TM, D_MODEL), lambda b, i: (b, i, 0)),
        out_shape=jax.ShapeDtypeStruct((B, nt * TM, D_MODEL), F32),
        compiler_params=_cparams(("arbitrary", "arbitrary")),
        name="merge",
    )(h, mod, dnf, dnb, z, yb, yc, gates, dn_norm, wb, wo)


def _ffn_kernel(h_ref, mod_ref, nw_ref, wg_ref, wu_ref, wd_ref, fw_ref, o_ref, *, final):
    h = h_ref[...]
    x = _norm_modulate(h, nw_ref[...], mod_ref[3:4, :], mod_ref[4:5, :]).astype(BF16)
    acc = None
    for c in range(0, D_FF, FF_CHUNK):
        g = _dot(x, wg_ref[:, c:c + FF_CHUNK])
        u = _dot(x, wu_ref[:, c:c + FF_CHUNK])
        part = _dot(((g * _sigmoid(g)) * u).astype(BF16), wd_ref[c:c + FF_CHUNK, :])
        acc = part if acc is None else acc + part
    out = h + mod_ref[5:6, :] * acc
    if final:
        ms = jnp.mean(out * out, axis=-1, keepdims=True)
        out = out * lax.rsqrt(ms + 1e-6) * fw_ref[...]
    o_ref[...] = out


def _ffn_call(h, mod, nw, wg, wu, wd, fw, seg_off, final):
    B, T, _ = h.shape
    nt = T // TM
    return pl.pallas_call(
        functools.partial(_ffn_kernel, final=final),
        grid=(B, nt),
        in_specs=[
            pl.BlockSpec((None, TM, D_MODEL), lambda b, i: (b, i, 0)),
            pl.BlockSpec((None, None, 6, D_MODEL), lambda b, i: _mod_index(b, i + seg_off)),
            pl.BlockSpec((1, D_MODEL), lambda b, i: (0, 0)),
            pl.BlockSpec((D_MODEL, D_FF), lambda b, i: (0, 0)),
            pl.BlockSpec((D_MODEL, D_FF), lambda b, i: (0, 0)),
            pl.BlockSpec((D_FF, D_MODEL), lambda b, i: (0, 0)),
            pl.BlockSpec((1, D_MODEL), lambda b, i: (0, 0)),
        ],
        out_specs=pl.BlockSpec((None, TM, D_MODEL), lambda b, i: (b, i, 0)),
        out_shape=jax.ShapeDtypeStruct((B, T, D_MODEL), F32),
        compiler_params=_cparams(("arbitrary", "arbitrary")),
        name="ffn",
    )(h, mod, nw, wg, wu, wd, fw)


def _reorder_w_in(w):
    parts = [w[:, 0:1536], w[:, 1536:2048], w[:, 2064:2576], w[:, 2576:3088], w[:, 3088:4624], w[:, 4624:7696],
             w[:, 2048:2064], jnp.zeros((w.shape[0], IN_COLS_PAD - 7696), w.dtype)]
    return jnp.concatenate(parts, axis=1).astype(BF16)


def _block_diag(w):
    g, n, _ = w.shape
    return jnp.einsum('gij,gh->gihj', w, jnp.eye(g, dtype=w.dtype)).reshape(g * n, g * n)


def _lru_gate_weights(wa, ba, wi, bi):
    half = LRU_G // 2
    ws, bs = [], []
    for p in range(2):
        grp = slice(p * half, (p + 1) * half)
        ch = slice(p * MIX_W // 2, (p + 1) * MIX_W // 2)
        cols, bcols = [], []
        for d in range(2):
            cols += [_block_diag(wa[d, grp]), _block_diag(wi[d, grp])]
            bcols += [ba[d, ch], bi[d, ch]]
        ws.append(jnp.concatenate(cols, axis=1))
        bs.append(jnp.concatenate(bcols)[None, :])
    return jnp.stack(ws).astype(BF16), jnp.stack(bs)


def _rope_tables(seq):
    rows = seq // GRID_W
    rowp = jnp.repeat(jnp.arange(rows, dtype=F32), GRID_W)
    colp = jnp.tile(jnp.arange(GRID_W, dtype=F32), rows)
    inv = ROPE_BASE ** (-jnp.arange(ROPE_NF, dtype=F32) / ROPE_NF)
    ar, ac = rowp[:, None] * inv, colp[:, None] * inv
    cs = jnp.concatenate([jnp.cos(ar), jnp.cos(ar), jnp.cos(ac), jnp.cos(ac)], axis=1)
    sn = jnp.concatenate([-jnp.sin(ar), jnp.sin(ar), -jnp.sin(ac), jnp.sin(ac)], axis=1)
    cs = jnp.concatenate([jnp.ones((CTX_LEN, 64), F32), cs], axis=0)
    sn = jnp.concatenate([jnp.zeros((CTX_LEN, 64), F32), sn], axis=0)
    return jnp.tile(cs, (1, 2)), jnp.tile(sn, (1, 2))


def kernel(x, c, ctx, c_ctx, w_mod, b_mod, norm_mix, norm_ffn, w_in, dn_conv, dn_a_log, dn_dt_bias, dn_norm, lru_conv_w, lru_conv_b, lru_wa, lru_ba, lru_wi, lru_bi, lru_lambda, da_lambda, da_norm, w_branch, w_out, w_ffn_gate, w_ffn_up, w_ffn_down, norm_final):
    B, S, D = x.shape
    depth = w_mod.shape[0]
    rows = jnp.concatenate([c, c_ctx[None, :], jnp.zeros((16 - B - 1, D), F32)], axis=0)
    mod = _mod_call(rows, w_mod, b_mod).reshape(depth, 16, 6, D)
    cos_t, sin_t = _rope_tables(S)
    h = jnp.concatenate([ctx, x], axis=1)
    for l in range(depth):
        last = l == depth - 1
        t_off = 1 if last else 0
        lam_init = 0.8 - 0.6 * math.exp(-0.3 * l)
        mod_l = jnp.stack([jnp.broadcast_to(mod[l, B][None], (B, 6, D)), mod[l, :B]], axis=1)
        dn_qkv, dn_z, lru_x, lru_y, da_qkv, gates, ab = _inproj_call(
            h, mod_l, norm_mix[l][None, :], _reorder_w_in(w_in[l]))
        dn_f, dn_b = _dn_seq_call(_dn_chunk_call(dn_qkv, ab, dn_conv[l], dn_a_log[l], dn_dt_bias[l]))
        wg, bg = _lru_gate_weights(lru_wa[l], lru_ba[l], lru_wi[l], lru_bi[l])
        yb = _lru_call(lru_x, lru_y, lru_conv_w[l], lru_conv_b[l][None, :], wg, bg, lru_lambda[l])
        yc = _attn_call(da_qkv, cos_t, sin_t, da_lambda[l], da_norm[l][None, :], lam_init, t_off)
        h = _merge_call(h, mod_l, dn_f, dn_b, dn_z, yb, yc, gates, dn_norm[l][None, :],
                        w_branch[l].astype(BF16), w_out[l].astype(BF16), t_off)
        h = _ffn_call(h, mod_l, norm_ffn[l][None, :], w_ffn_gate[l].astype(BF16), w_ffn_up[l].astype(BF16),
                      w_ffn_down[l].astype(BF16), norm_final[None, :], t_off, last)
    return h
```

```python
import functools
import math

import jax
import jax.numpy as jnp
import numpy as np
from jax import lax
from jax.experimental import pallas as pl
from jax.experimental.pallas import tpu as pltpu

F32 = jnp.float32
BF16 = jnp.bfloat16

D_MODEL = 1024
CTX_LEN = 256
GRID_W = 64
MIX_W = 512
N_HEAD = 4
HEAD_W = 128
DN_CHUNK = 64
LRU_C = 8.0
LRU_G = 8
ROPE_BASE = 10000.0
ROPE_NF = 16
D_FF = 2816
TM = 256
FF_CHUNK = 512
VT_ROWS = HEAD_W + 16
ATTN_HP = 2
DN_SEQ_BB = 2
VMEM_LIMIT = 56 * 1024 * 1024

C_DNQKV, C_DNZ, C_LRUX, C_LRUY, C_DAQKV, C_GATE, C_AB = 0, 1536, 2048, 2560, 3072, 4608, 7680
IN_COLS_PAD = 7808


def _cparams(sem):
    return pltpu.CompilerParams(dimension_semantics=sem, vmem_limit_bytes=VMEM_LIMIT)


def _sigmoid(x):
    return 1.0 / (1.0 + jnp.exp(-x))


def _softplus(x):
    return jnp.maximum(x, 0.0) + jnp.log(1.0 + jnp.exp(-jnp.abs(x)))


def _dot(a, b):
    return jnp.dot(a, b, preferred_element_type=F32)


def _dot_nt(a, b):
    return lax.dot_general(a, b, (((1,), (1,)), ((), ())), preferred_element_type=F32)


def _dot_tn(a, b):
    return lax.dot_general(a, b, (((0,), (0,)), ((), ())), preferred_element_type=F32)


def _mod_kernel(s_ref, w_ref, b_ref, o_ref):
    s = s_ref[...]
    s = s * _sigmoid(s)
    o_ref[...] = jnp.dot(s, w_ref[...], preferred_element_type=F32,
                         precision=lax.Precision.HIGHEST) + b_ref[...]


def _mod_call(rows, w_mod, b_mod):
    depth = w_mod.shape[0]
    n6 = w_mod.shape[2]
    tn = 1024
    return pl.pallas_call(
        _mod_kernel,
        grid=(depth, n6 // tn),
        in_specs=[
            pl.BlockSpec((16, D_MODEL), lambda l, j: (0, 0)),
            pl.BlockSpec((None, D_MODEL, tn), lambda l, j: (l, 0, j)),
            pl.BlockSpec((None, 1, tn), lambda l, j: (l, 0, j)),
        ],
        out_specs=pl.BlockSpec((None, 16, tn), lambda l, j: (l, 0, j)),
        out_shape=jax.ShapeDtypeStruct((depth, 16, n6), F32),
        compiler_params=_cparams(("arbitrary", "arbitrary")),
        name="adaln_mod",
    )(rows, w_mod, b_mod.reshape(depth, 1, n6))


def _mod_index(b, i):
    return (b, jnp.minimum(i, 1), 0, 0)


def _norm_modulate(x, nw, shift, scale, eps=1e-6):
    ms = jnp.mean(x * x, axis=-1, keepdims=True)
    y = x * lax.rsqrt(ms + eps) * nw
    return y * (1.0 + scale) + shift


_INPROJ_SEGS = (
    (C_DNQKV, 1536), (C_DNZ, 512), (C_LRUX, 512), (C_LRUY, 512), (C_DAQKV, 1536), (C_GATE, 3072), (C_AB, 128))


def _inproj_kernel(h_ref, mod_ref, nw_ref, w_ref, *out_refs):
    u = _norm_modulate(h_ref[...], nw_ref[...], mod_ref[0:1, :], mod_ref[1:2, :]).astype(BF16)
    for (c0, width), o_ref in zip(_INPROJ_SEGS, out_refs):
        step = min(width, 512)
        for c in range(0, width, step):
            o_ref[:, c:c + step] = _dot(u, w_ref[:, c0 + c:c0 + c + step]).astype(o_ref.dtype)


def _inproj_call(h, mod, nw, w):
    B, T, _ = h.shape
    nt = T // TM
    dts = (BF16, BF16, BF16, BF16, BF16, BF16, F32)
    return pl.pallas_call(
        _inproj_kernel,
        grid=(B, nt),
        in_specs=[
            pl.BlockSpec((None, TM, D_MODEL), lambda b, i: (b, i, 0)),
            pl.BlockSpec((None, None, 6, D_MODEL), _mod_index),
            pl.BlockSpec((1, D_MODEL), lambda b, i: (0, 0)),
            pl.BlockSpec((D_MODEL, IN_COLS_PAD), lambda b, i: (0, 0)),
        ],
        out_specs=[pl.BlockSpec((None, TM, wd), lambda b, i: (b, i, 0)) for _, wd in _INPROJ_SEGS],
        out_shape=[jax.ShapeDtypeStruct((B, T, wd), dt) for (_, wd), dt in zip(_INPROJ_SEGS, dts)],
        compiler_params=_cparams(("arbitrary", "arbitrary")),
        name="in_proj",
    )(h, mod, nw, w)


def _halo_valid(i, nt):
    vp = jnp.where(i >= 2, 1.0, 0.0).astype(F32)
    vn = jnp.where(jnp.logical_and(i >= 1, i <= nt - 2), 1.0, 0.0).astype(F32)
    return vp, vn


def _dn_constants():
    r = np.arange(TM)[:, None]
    c = np.arange(TM)[None, :]
    same = (r // DN_CHUNK) == (c // DN_CHUNK)
    shifts = np.stack([c == r - 1, c == r + 1, c == r + 2]).astype(np.float32)
    sums = np.concatenate([same & (c <= r), same], axis=0).astype(np.float32)
    masks = np.stack([same & (c <= r), same & (c >= r), same & (c < r), same & (c > r),
                      (r // 16) == (c // 16), r == c]).astype(np.float32)
    return jnp.asarray(shifts, BF16), jnp.asarray(sums, BF16), jnp.asarray(masks, F32)


def _dn_chunk_kernel(x_ref, xp_ref, xn_ref, ab_ref, cw_ref, al_ref, dtb_ref, sh_ref, sm_ref, mk_ref,
                     wf_ref, uf_ref, qf_ref, kf_ref, qkf_ref, wb_ref, ub_ref, qb_ref, kb_ref, qkb_ref, ge_ref,
                     s_ref, gc_ref, bt_ref, gct_ref, gt_ref, *, nt):
    i = pl.program_id(1)
    vp, vn = _halo_valid(i, nt)
    row8 = lax.broadcasted_iota(jnp.int32, (8, HEAD_W), 0)
    for cb2 in range(3 * N_HEAD // 2):
        c2 = slice(cb2 * 2 * HEAD_W, (cb2 + 1) * 2 * HEAD_W)
        xb = x_ref[:, c2]
        xm1, xp1, xp2 = _dot(sh_ref[0], xb), _dot(sh_ref[1], xb), _dot(sh_ref[2], xb)
        xf = xb.astype(F32)
        for half in range(2):
            cb = cb2 * 2 + half
            cols = slice(cb * HEAD_W, (cb + 1) * HEAD_W)
            hc = slice(half * HEAD_W, (half + 1) * HEAD_W)
            w = cw_ref[:, cols]
            y = w[0:1, :] * xm1[:, hc] + w[1:2, :] * xf[:, hc] + w[2:3, :] * xp1[:, hc] + w[3:4, :] * xp2[:, hc]
            prev = xp_ref[:, cols].astype(F32)[15:16, :] * vp
            nxt = xn_ref[:, cols].astype(F32)
            n0, n1 = nxt[0:1, :] * vn, nxt[1:2, :] * vn
            top = jnp.where(row8 == 0, w[0:1, :] * prev, 0.0)
            bot = jnp.where(row8 == 6, w[3:4, :] * n0, jnp.where(row8 == 7, w[2:3, :] * n0 + w[3:4, :] * n1, 0.0))
            y = jnp.concatenate([y[0:8, :] + top, y[8:TM - 8, :], y[TM - 8:TM, :] + bot], axis=0)
            s = y * _sigmoid(y)
            if cb < 2 * N_HEAD:
                ss = jnp.sum(s * s, axis=-1, keepdims=True)
                s = s * (lax.rsqrt(ss + 1e-6) * (HEAD_W ** -0.5 if cb < N_HEAD else 1.0))
            s_ref[cb] = s

    ab = ab_ref[...]
    lane = lax.broadcasted_iota(jnp.int32, (TM, 128), 1)
    g = -jnp.exp(al_ref[...]) * _softplus(ab + dtb_ref[...])
    g1 = g.astype(BF16)
    r1 = g - g1.astype(F32)
    g2 = r1.astype(BF16)
    g3 = (r1 - g2.astype(F32)).astype(BF16)
    sums = _dot(sm_ref[...], jnp.concatenate([g1, g2, g3], axis=1))
    sums = sums[:, 0:128] + sums[:, 128:256] + sums[:, 256:384]
    g_f = sums[0:TM, :]
    gtot = sums[TM:2 * TM, :]
    gc = jnp.where(lane < N_HEAD, g_f, gtot - g_f + g)
    gc_ref[...] = gc
    bt_ref[...] = _sigmoid(ab)
    gt_ref[...] = gtot
    gct_ref[...] = gc.T
    ge_ref[...] = jnp.exp(gtot)
    out_refs = ((wf_ref, uf_ref, qf_ref, kf_ref, qkf_ref), (wb_ref, ub_ref, qb_ref, kb_ref, qkb_ref))
    sub8 = lax.broadcasted_iota(jnp.int32, (8, TM), 0)

    probs = [(h, d) for h in range(N_HEAD) for d in range(2)]
    kk, qk0 = {}, {}
    for h in range(N_HEAD):
        kb16 = s_ref[N_HEAD + h].astype(BF16)
        kk[h] = _dot_nt(kb16, kb16)
        qk0[h] = _dot_nt(s_ref[h].astype(BF16), kb16)
    tmat, dblk, lo_b, rhs_b = {}, {}, {}, {}
    for h, d in probs:
        q = s_ref[h]
        k = s_ref[N_HEAD + h]
        v = s_ref[2 * N_HEAD + h]
        idx = d * N_HEAD + h
        sel = lane == idx
        gc_col = jnp.sum(jnp.where(sel, gc_ref[...], 0.0), axis=1, keepdims=True)
        gt_col = jnp.sum(jnp.where(sel, gt_ref[...], 0.0), axis=1, keepdims=True)
        bt_col = jnp.sum(jnp.where(lane == idx + 2 * N_HEAD, bt_ref[...], 0.0), axis=1, keepdims=True)
        gc_row = jnp.sum(jnp.where(sub8 == idx, gct_ref[0:8, :], 0.0), axis=0, keepdims=True)
        e = jnp.exp(jnp.minimum(gc_col - gc_row, 0.0))
        a = kk[h] * (e * mk_ref[2 + d] * bt_col)
        eg = jnp.exp(gc_col)
        _, _, q_r, k_r, qk_r = out_refs[d]
        q_r[h] = (q * eg).astype(BF16)
        k_r[h] = (k * jnp.exp(gt_col - gc_col)).astype(BF16)
        qk_r[h] = (qk0[h] * (e * mk_ref[d])).astype(BF16)
        dm = a * mk_ref[4]
        tmat[h, d] = mk_ref[5] - dm
        lo_b[h, d] = (a - dm).astype(BF16)
        rhs_b[h, d] = jnp.concatenate([k * (bt_col * eg), v * bt_col], axis=1).astype(BF16)
        dblk[h, d] = dm.astype(BF16)
    pw = {pk: _dot(dblk[pk], dblk[pk]) for pk in probs}
    for m in range(3):
        for pk in probs:
            pb = pw[pk].astype(BF16)
            tmat[pk] = tmat[pk] + _dot(tmat[pk].astype(BF16), pb)
            if m < 2:
                pw[pk] = _dot(pb, pb)
    x0, mt, m2 = {}, {}, {}
    for pk in probs:
        tb = tmat[pk].astype(BF16)
        x0[pk] = _dot(tb, rhs_b[pk])
        mt[pk] = _dot(tb, lo_b[pk]).astype(BF16)
    for pk in probs:
        m2[pk] = _dot(mt[pk], mt[pk]).astype(BF16)
    ys = {pk: x0[pk] + _dot(m2[pk], x0[pk].astype(BF16)) for pk in probs}
    for h, d in probs:
        x = ys[h, d] - _dot(mt[h, d], ys[h, d].astype(BF16))
        w_r, u_r = out_refs[d][0], out_refs[d][1]
        w_r[h] = x[:, :HEAD_W].astype(BF16)
        u_r[h] = x[:, HEAD_W:]


def _dn_chunk_call(qkv, ab, conv_w, a_log, dt_bias):
    B, T, W = qkv.shape
    nt = T // TM
    hb = TM // 16
    last16 = T // 16 - 1
    al = jnp.zeros((1, 128), F32).at[0, :2 * N_HEAD].set(a_log.reshape(-1))
    dtb = jnp.zeros((1, 128), F32).at[0, :2 * N_HEAD].set(dt_bias.reshape(-1))
    shifts, sums, masks = _dn_constants()
    hspec = lambda w: pl.BlockSpec((None, N_HEAD, TM, w), lambda b, i: (b, 0, i, 0))
    hshape = lambda w, dt: jax.ShapeDtypeStruct((B, N_HEAD, T, w), dt)
    const = lambda a: pl.BlockSpec(a.shape, lambda b, i: (0,) * a.ndim)
    per_dir_specs = [hspec(HEAD_W), hspec(HEAD_W), hspec(HEAD_W), hspec(HEAD_W), hspec(TM)]
    per_dir_shapes = [hshape(HEAD_W, BF16), hshape(HEAD_W, F32), hshape(HEAD_W, BF16), hshape(HEAD_W, BF16),
                      hshape(TM, BF16)]
    return pl.pallas_call(
        functools.partial(_dn_chunk_kernel, nt=nt),
        grid=(B, nt),
        in_specs=[
            pl.BlockSpec((None, TM, W), lambda b, i: (b, i, 0)),
            pl.BlockSpec((None, 16, W), lambda b, i: (b, jnp.maximum(i * hb - 1, 0), 0)),
            pl.BlockSpec((None, 16, W), lambda b, i: (b, jnp.minimum((i + 1) * hb, last16), 0)),
            pl.BlockSpec((None, TM, 128), lambda b, i: (b, i, 0)),
            pl.BlockSpec((4, W), lambda b, i: (0, 0)),
            pl.BlockSpec((1, 128), lambda b, i: (0, 0)),
            pl.BlockSpec((1, 128), lambda b, i: (0, 0)),
            const(shifts), const(sums), const(masks),
        ],
        out_specs=per_dir_specs + per_dir_specs + [pl.BlockSpec((None, TM, 128), lambda b, i: (b, i, 0))],
        out_shape=per_dir_shapes + per_dir_shapes + [jax.ShapeDtypeStruct((B, T, 128), F32)],
        scratch_shapes=[
            pltpu.VMEM((3 * N_HEAD, TM, HEAD_W), F32),
            pltpu.VMEM((TM, 128), F32), pltpu.VMEM((TM, 128), F32),
            pltpu.VMEM((128, TM), F32), pltpu.VMEM((TM, 128), F32),
        ],
        compiler_params=_cparams(("arbitrary", "arbitrary")),
        name="dn_chunk",
    )(qkv, qkv, qkv, ab, conv_w, al, dtb, shifts, sums, masks)


def _dn_seq_kernel(wf_ref, uf_ref, qf_ref, kf_ref, qkf_ref, gef_ref, wb_ref, ub_ref, qb_ref, kb_ref, qkb_ref,
                   geb_ref, of_ref, ob_ref, st_ref):
    C = DN_CHUNK
    nc = TM // C

    @pl.when(pl.program_id(1) == 0)
    def _():
        st_ref[...] = jnp.zeros_like(st_ref)

    ins = ((wf_ref, uf_ref, qf_ref, kf_ref, qkf_ref, gef_ref, of_ref),
           (wb_ref, ub_ref, qb_ref, kb_ref, qkb_ref, geb_ref, ob_ref))
    chains = [(d, bb, h) for d in range(2) for bb in range(DN_SEQ_BB) for h in range(N_HEAD)]
    for ci in range(nc):
        st, sb, ws, qs, vb = {}, {}, {}, {}, {}
        for ch in chains:
            d, bb, h = ch
            c = ci if d == 0 else nc - 1 - ci
            rows = slice(c * C, (c + 1) * C)
            st[ch] = st_ref[(d * DN_SEQ_BB + bb) * N_HEAD + h]
            sb[ch] = st[ch].astype(BF16)
            ws[ch] = _dot(ins[d][0][bb, h, rows, :], sb[ch])
            qs[ch] = _dot(ins[d][2][bb, h, rows, :], sb[ch])
        for ch in chains:
            d, bb, h = ch
            c = ci if d == 0 else nc - 1 - ci
            rows = slice(c * C, (c + 1) * C)
            vb[ch] = (ins[d][1][bb, h, rows, :] - ws[ch]).astype(BF16)
        for ch in chains:
            d, bb, h = ch
            c = ci if d == 0 else nc - 1 - ci
            rows = slice(c * C, (c + 1) * C)
            _, _, _, k_r, qk_r, ge_r, o_r = ins[d]
            si = d * N_HEAD + h
            ge = ge_r[bb, c * C:c * C + 8, :][0:1, si:si + 1]
            o_r[bb, rows, h * HEAD_W:(h + 1) * HEAD_W] = qs[ch] + _dot(qk_r[bb, h, rows, c * C:(c + 1) * C], vb[ch])
            st_ref[(d * DN_SEQ_BB + bb) * N_HEAD + h] = ge * st[ch] + _dot_tn(k_r[bb, h, rows, :], vb[ch])


def _dn_seq_call(chunk_outs):
    wf, uf, qf, kf, qkf, wb, ub, qb, kb, qkb, ge = chunk_outs
    B, _, T, _ = wf.shape
    nt = T // TM
    bb = DN_SEQ_BB
    tf = lambda s: s
    tb = lambda s: jnp.where(s == 0, 0, nt - s)
    hspec = lambda w, t: pl.BlockSpec((bb, N_HEAD, TM, w), lambda b, s: (b, 0, t(s), 0))
    gspec = lambda t: pl.BlockSpec((bb, TM, 128), lambda b, s: (b, t(s), 0))
    dir_specs = lambda t: [hspec(HEAD_W, t), hspec(HEAD_W, t), hspec(HEAD_W, t), hspec(HEAD_W, t), hspec(TM, t),
                           gspec(t)]
    ospec = lambda t: pl.BlockSpec((bb, TM, MIX_W), lambda b, s: (b, t(s), 0))
    return pl.pallas_call(
        _dn_seq_kernel,
        grid=(B // bb, nt),
        in_specs=dir_specs(tf) + dir_specs(tb),
        out_specs=[ospec(tf), ospec(tb)],
        out_shape=[jax.ShapeDtypeStruct((B, T, MIX_W), F32)] * 2,
        scratch_shapes=[pltpu.VMEM((2 * bb * N_HEAD, HEAD_W, HEAD_W), F32)],
        compiler_params=_cparams(("arbitrary", "arbitrary")),
        name="dn_seq",
    )(wf, uf, qf, kf, qkf, ge, wb, ub, qb, kb, qkb, ge)


def _gelu_tanh(x):
    return 0.5 * x * (1.0 + jnp.tanh(math.sqrt(2.0 / math.pi) * (x + 0.044715 * (x * x * x))))


def _lru_kernel(x_ref, y_ref, cw_ref, cb_ref, wg_ref, bg_ref, lam_ref, sh_ref, o_ref, hf_ref, xc_ref, *, nt):
    CW = x_ref.shape[1]
    T = x_ref.shape[0]
    G = TM // 8
    row8 = lax.broadcasted_iota(jnp.int32, (8, CW), 0)
    sub = lax.broadcasted_iota(jnp.int32, (G, 8, CW), 1)

    def conv(t):
        base = pl.multiple_of(t * TM, TM)
        xb = x_ref[pl.ds(base, TM), :]
        xp = x_ref[pl.ds(pl.multiple_of(jnp.maximum(base - 16, 0), 16), 16), :].astype(F32)
        xn = x_ref[pl.ds(pl.multiple_of(jnp.minimum(base + TM, T - 16), 16), 16), :].astype(F32)
        vp, vn = _halo_valid(t, nt)
        w = cw_ref[...]
        y = (w[0:1, :] * _dot(sh_ref[0], xb) + w[1:2, :] * xb.astype(F32)
             + w[2:3, :] * _dot(sh_ref[1], xb) + w[3:4, :] * _dot(sh_ref[2], xb))
        n0, n1 = xn[0:1, :] * vn, xn[1:2, :] * vn
        top = jnp.where(row8 == 0, w[0:1, :] * (xp[15:16, :] * vp), 0.0)
        bot = jnp.where(row8 == 6, w[3:4, :] * n0, jnp.where(row8 == 7, w[2:3, :] * n0 + w[3:4, :] * n1, 0.0))
        return jnp.concatenate([y[0:8, :] + top, y[8:TM - 8, :], y[TM - 8:TM, :] + bot], axis=0) + cb_ref[...]

    def coeffs(xc, d):
        z = _dot(xc.astype(BF16), wg_ref[:, d * 2 * CW:(d + 1) * 2 * CW]) + bg_ref[:, d * 2 * CW:(d + 1) * 2 * CW]
        rg = _sigmoid(z[:, :CW])
        ig = _sigmoid(z[:, CW:])
        log_a = (-LRU_C * _softplus(-lam_ref[d:d + 1, :])) * rg
        a = jnp.exp(log_a)
        b = jnp.sqrt(1.0 - a * a) * (ig * xc)
        return a, b

    def scan_tile(a, b, h_in, d):
        a = a.reshape(G, 8, CW)
        b = b.reshape(G, 8, CW)
        for s in (1, 2, 4):
            if d == 0:
                a_s, b_s, valid = pltpu.roll(a, s, 1), pltpu.roll(b, s, 1), sub >= s
            else:
                a_s, b_s, valid = pltpu.roll(a, 8 - s, 1), pltpu.roll(b, 8 - s, 1), sub < 8 - s
            b = jnp.where(valid, a * b_s + b, b)
            a = jnp.where(valid, a * a_s, a)
        hs = [None] * G
        h = h_in
        for g in (range(G) if d == 0 else range(G - 1, -1, -1)):
            hg = b[g] + a[g] * h
            hs[g] = hg
            h = hg[7:8, :] if d == 0 else hg[0:1, :]
        return jnp.concatenate(hs, axis=0), h

    def fwd(t, h):
        rows = pl.ds(pl.multiple_of(t * TM, TM), TM)
        xc = conv(t)
        xc_ref[rows, :] = xc
        a, b = coeffs(xc, 0)
        hf, h = scan_tile(a, b, h, 0)
        hf_ref[rows, :] = hf
        return h

    lax.fori_loop(0, nt, fwd, jnp.zeros((1, CW), F32))

    def bwd(s, h):
        t = jnp.where(s == 0, 0, nt - s)
        rows = pl.ds(pl.multiple_of(t * TM, TM), TM)
        a, b = coeffs(xc_ref[rows, :], 1)
        hb, h = scan_tile(a, b, h, 1)
        y = y_ref[rows, :].astype(F32)
        o_ref[rows, :] = ((hf_ref[rows, :] + hb) * _gelu_tanh(y)).astype(o_ref.dtype)
        return h

    lax.fori_loop(0, nt, bwd, jnp.zeros((1, CW), F32))


def _lru_call(x, y, conv_w, conv_b, wg, bg, lam):
    B, T, W = x.shape
    cw = W // 2
    nt = T // TM
    shifts = _dn_constants()[0]
    return pl.pallas_call(
        functools.partial(_lru_kernel, nt=nt),
        grid=(B, 2),
        in_specs=[
            pl.BlockSpec((None, T, cw), lambda b, p: (b, 0, p)),
            pl.BlockSpec((None, T, cw), lambda b, p: (b, 0, p)),
            pl.BlockSpec((4, cw), lambda b, p: (0, p)),
            pl.BlockSpec((1, cw), lambda b, p: (0, p)),
            pl.BlockSpec((None, cw, 4 * cw), lambda b, p: (p, 0, 0)),
            pl.BlockSpec((None, 1, 4 * cw), lambda b, p: (p, 0, 0)),
            pl.BlockSpec((2, cw), lambda b, p: (0, p)),
            pl.BlockSpec(shifts.shape, lambda b, p: (0, 0, 0)),
        ],
        out_specs=pl.BlockSpec((None, T, cw), lambda b, p: (b, 0, p)),
        out_shape=jax.ShapeDtypeStruct((B, T, W), BF16),
        scratch_shapes=[pltpu.VMEM((T, cw), F32), pltpu.VMEM((T, cw), F32)],
        compiler_params=_cparams(("arbitrary", "arbitrary")),
        name="rglru",
    )(x, y, conv_w, conv_b, wg, bg, lam, shifts)


def _rope(x, cs, sn, first_half):
    swapped = jnp.where(first_half, pltpu.roll(x, 112, 1), pltpu.roll(x, 16, 1))
    return x * cs + swapped * sn


def _attn_kernel(*refs, lam_init, has_ctx, nt):
    (qa_ref, qb_ref, k_ref, v_ref, ca_ref, sa_ref, cb_ref, sb_ref, ck_ref, sk_ref, lv_ref, nw_ref) = refs[:12]
    outs = refs[12:12 + 1 + has_ctx]
    kr_ref, vt_ref, qt_ref, st0_ref, st1_ref, acc_ref = refs[12 + 1 + has_ctx:]
    ol_ref = outs[-1]
    step = pl.program_id(2)
    lane = lax.broadcasted_iota(jnp.int32, (TM, HEAD_W), 1)
    first_half = (lane % 32) < 16
    heads = [(hh, slice(hh * HEAD_W, (hh + 1) * HEAD_W)) for hh in range(ATTN_HP)]
    q_srcs = ((qa_ref, ca_ref, sa_ref), (qb_ref, cb_ref, sb_ref))

    @pl.when(step == 0)
    def _():
        extra = (lax.broadcasted_iota(jnp.int32, (VT_ROWS - HEAD_W, TM), 0) == 0).astype(BF16)

        def prep_kv(t, carry):
            rows = pl.ds(pl.multiple_of(t * TM, TM), TM)
            for hh, cols in heads:
                kr_ref[hh, rows, :] = _rope(k_ref[rows, cols].astype(F32), ck_ref[rows, :], sk_ref[rows, :],
                                            first_half).astype(BF16)
                vt_ref[hh, 0:HEAD_W, rows] = v_ref[rows, cols].astype(F32).T.astype(BF16)
                vt_ref[hh, HEAD_W:VT_ROWS, rows] = extra
            return carry
        lax.fori_loop(0, nt, prep_kv, 0)

    lv = lv_ref[...]
    lam = (jnp.exp(jnp.sum(lv[0:1, :] * lv[1:2, :], axis=-1, keepdims=True))
           - jnp.exp(jnp.sum(lv[2:3, :] * lv[3:4, :], axis=-1, keepdims=True)) + lam_init)

    def run(chains, n_tiles):
        for qi, hh in chains:
            q_ref, c_ref, s_ref = q_srcs[qi]
            q = _rope(q_ref[:, heads[hh][1]].astype(F32), c_ref[...], s_ref[...], first_half) * (
                64 ** -0.5 * math.log2(math.e))
            qt_ref[qi * ATTN_HP + hh] = jnp.concatenate(
                [jnp.where(lane < 64, q, 0.0).T, jnp.where(lane >= 64, q, 0.0).T], axis=1).astype(BF16)
            acc_ref[qi * ATTN_HP + hh] = jnp.zeros((VT_ROWS, 2 * TM), F32)

        def scores(j, st_ref):
            mx = {}
            for qi, hh in chains:
                ci = qi * ATTN_HP + hh
                st = _dot(kr_ref[hh, j * TM:(j + 1) * TM, :], qt_ref[ci])
                st_ref[ci] = st
                mx[ci] = jnp.max(st, axis=0, keepdims=True)
            return mx

        def softmax_pv(j, st_ref, ms, mx):
            out = {}
            for qi, hh in chains:
                ci = qi * ATTN_HP + hh
                m_new = jnp.maximum(ms[ci], mx[ci])
                alpha = jnp.exp2(ms[ci] - m_new)
                for half in range(2):
                    hc = slice(half * TM, (half + 1) * TM)
                    p = jnp.exp2(st_ref[ci, :, hc] - m_new[:, hc]).astype(BF16)
                    acc_ref[ci, :, hc] = (alpha[:, hc] * acc_ref[ci, :, hc]
                                          + _dot(vt_ref[hh, :, j * TM:(j + 1) * TM], p))
                out[ci] = m_new
            return out

        bufs = (st0_ref, st1_ref)
        mx = scores(0, st0_ref)
        ms = {qi * ATTN_HP + hh: jnp.full((1, 2 * TM), -1e30, F32) for qi, hh in chains}
        for j in range(n_tiles):
            mx_next = scores(j + 1, bufs[(j + 1) % 2]) if j + 1 < n_tiles else None
            ms = softmax_pv(j, bufs[j % 2], ms, mx)
            mx = mx_next
        res = {}
        for qi, hh in chains:
            ci = qi * ATTN_HP + hh
            on = acc_ref[ci, 0:HEAD_W, :] * (1.0 / acc_ref[ci, HEAD_W:HEAD_W + 1, :])
            o = (on[:, :TM] - lam * on[:, TM:]).T
            ms_o = jnp.mean(o * o, axis=-1, keepdims=True)
            res[qi, hh] = (o * lax.rsqrt(ms_o + 1e-5) * nw_ref[...] * (1.0 - lam_init)).astype(ol_ref.dtype)
        return res

    def latent():
        res = run([(qi, hh) for qi in range(2) for hh in range(ATTN_HP)], nt)
        for (qi, hh), o in res.items():
            ol_ref[qi * TM:(qi + 1) * TM, heads[hh][1]] = o

    if has_ctx:
        @pl.when(step == 0)
        def _():
            res = run([(0, hh) for hh in range(ATTN_HP)], 1)
            for (_, hh), o in res.items():
                outs[0][:, heads[hh][1]] = o

        pl.when(step > 0)(latent)
    else:
        latent()


def _attn_call(qkv, cos_t, sin_t, lam_vecs, sub_norm, lam_init, has_ctx):
    B, T, _ = qkv.shape
    nt = T // TM
    assert (nt - 1) % 2 == 0, "latent query tiles are processed in pairs"
    hc = 1 if has_ctx else 0
    nsteps = (nt - 1) // 2 + hc
    pw = ATTN_HP * HEAD_W
    npair = N_HEAD // ATTN_HP
    nch = 2 * ATTN_HP
    ta = lambda s: jnp.maximum(2 * (s - hc) + 1, 0)
    tb = lambda s: jnp.maximum(2 * (s - hc) + 2, 1)
    qspec = lambda t: pl.BlockSpec((None, TM, pw), lambda b, h, s: (b, t(s), h))
    tspec = lambda t: pl.BlockSpec((TM, HEAD_W), lambda b, h, s: (t(s), 0))
    out_specs = [pl.BlockSpec((None, 2 * TM, pw), lambda b, h, s: (b, jnp.maximum(s - hc, 0), h))]
    out_shape = [jax.ShapeDtypeStruct((B, T - TM, MIX_W), BF16)]
    if has_ctx:
        out_specs.insert(0, pl.BlockSpec((None, TM, pw), lambda b, h, s: (b, 0, h)))
        out_shape.insert(0, jax.ShapeDtypeStruct((B, TM, MIX_W), BF16))
    kern = functools.partial(_attn_kernel, lam_init=lam_init, has_ctx=hc, nt=nt)
    outs = pl.pallas_call(
        kern,
        grid=(B, npair, nsteps),
        in_specs=[
            qspec(ta), qspec(tb),
            pl.BlockSpec((None, T, pw), lambda b, h, s: (b, 0, npair + h)),
            pl.BlockSpec((None, T, pw), lambda b, h, s: (b, 0, 2 * npair + h)),
            tspec(ta), tspec(ta), tspec(tb), tspec(tb),
            pl.BlockSpec((T, HEAD_W), lambda b, h, s: (0, 0)),
            pl.BlockSpec((T, HEAD_W), lambda b, h, s: (0, 0)),
            pl.BlockSpec((4, 64), lambda b, h, s: (0, 0)),
            pl.BlockSpec((1, HEAD_W), lambda b, h, s: (0, 0)),
        ],
        out_specs=out_specs,
        out_shape=out_shape,
        scratch_shapes=[pltpu.VMEM((ATTN_HP, T, HEAD_W), BF16), pltpu.VMEM((ATTN_HP, VT_ROWS, T), BF16),
                        pltpu.VMEM((nch, HEAD_W, 2 * TM), BF16),
                        pltpu.VMEM((nch, TM, 2 * TM), F32), pltpu.VMEM((nch, TM, 2 * TM), F32),
                        pltpu.VMEM((nch, VT_ROWS, 2 * TM), F32)],
        compiler_params=_cparams(("arbitrary", "arbitrary", "arbitrary")),
        name="diff_attn",
    )(qkv, qkv, qkv, qkv, cos_t, sin_t, cos_t, sin_t, cos_t, sin_t, lam_vecs, sub_norm)
    return (outs[0], outs[1]) if has_ctx else (None, outs[0])


def _merge_kernel(h_ref, mod_ref, dnf_ref, dnb_ref, z_ref, yb_ref, *rest, t_off):
    ycl_ref, g_ref, dnw_ref, wb_ref, wo_ref, o_ref = rest[-6:]
    yc = ycl_ref[...]
    if len(rest) == 7:
        yc = jnp.where(pl.program_id(1) + t_off == 0, rest[0][...], yc)
    dnw = dnw_ref[...]
    ya_parts = []
    for hh in range(N_HEAD):
        cols = slice(hh * HEAD_W, (hh + 1) * HEAD_W)
        o = dnf_ref[:, cols] + dnb_ref[:, cols]
        ms = jnp.mean(o * o, axis=-1, keepdims=True)
        z = z_ref[:, cols].astype(F32)
        ya_parts.append(((o * lax.rsqrt(ms + 1e-6) * dnw) * (z * _sigmoid(z))).astype(BF16))
    ys = (jnp.concatenate(ya_parts, axis=1), yb_ref[...], yc)
    mix = None
    for n in range(3):
        up = _dot(ys[n], wb_ref[n])
        term = _sigmoid(g_ref[:, n * D_MODEL:(n + 1) * D_MODEL].astype(F32)) * up
        mix = term if mix is None else mix + term
    y = _dot(mix.astype(BF16), wo_ref[...])
    o_ref[...] = h_ref[...] + mod_ref[2:3, :] * y


def _merge_call(h, mod, dnf, dnb, z, yb, yc_ctx, yc_lat, gates, dn_norm, wb, wo, t_off):
    B, T, _ = h.shape
    nt = T // TM - t_off
    row = lambda w: pl.BlockSpec((None, TM, w), lambda b, i: (b, i + t_off, 0))
    yc_specs = [pl.BlockSpec((None, TM, MIX_W), lambda b, i: (b, jnp.maximum(i + t_off - 1, 0), 0))]
    yc_args = [yc_lat]
    if yc_ctx is not None:
        yc_specs.insert(0, pl.BlockSpec((None, TM, MIX_W), lambda b, i: (b, 0, 0)))
        yc_args.insert(0, yc_ctx)
    return pl.pallas_call(
        functools.partial(_merge_kernel, t_off=t_off),
        grid=(B, nt),
        in_specs=[
            row(D_MODEL),
            pl.BlockSpec((None, None, 6, D_MODEL), lambda b, i: _mod_index(b, i + t_off)),
            row(MIX_W), row(MIX_W), row(MIX_W), row(MIX_W), *yc_specs, row(3 * D_MODEL),
            pl.BlockSpec((1, HEAD_W), lambda b, i: (0, 0)),
            pl.BlockSpec((3, MIX_W, D_MODEL), lambda b, i: (0, 0, 0)),
            pl.BlockSpec((D_MODEL, D_MODEL), lambda b, i: (0, 0)),
        ],
        out_specs=pl.BlockSpec((None, TM, D_MODEL), lambda b, i: (b, i, 0)),
        out_shape=jax.ShapeDtypeStruct((B, nt * TM, D_MODEL), F32),
        compiler_params=_cparams(("arbitrary", "arbitrary")),
        name="merge",
    )(h, mod, dnf, dnb, z, yb, *yc_args, gates, dn_norm, wb, wo)


def _ffn_kernel(h_ref, mod_ref, nw_ref, wg_ref, wu_ref, wd_ref, fw_ref, o_ref, *, final):
    h = h_ref[...]
    x = _norm_modulate(h, nw_ref[...], mod_ref[3:4, :], mod_ref[4:5, :]).astype(BF16)
    acc = None
    for c in range(0, D_FF, FF_CHUNK):
        ce = min(c + FF_CHUNK, D_FF)
        g = _dot(x, wg_ref[:, c:ce])
        u = _dot(x, wu_ref[:, c:ce])
        part = _dot(((g * _sigmoid(g)) * u).astype(BF16), wd_ref[c:ce, :])
        acc = part if acc is None else acc + part
    out = h + mod_ref[5:6, :] * acc
    if final:
        ms = jnp.mean(out * out, axis=-1, keepdims=True)
        out = out * lax.rsqrt(ms + 1e-6) * fw_ref[...]
    o_ref[...] = out


def _ffn_call(h, mod, nw, wg, wu, wd, fw, seg_off, final):
    B, T, _ = h.shape
    nt = T // TM
    return pl.pallas_call(
        functools.partial(_ffn_kernel, final=final),
        grid=(B, nt),
        in_specs=[
            pl.BlockSpec((None, TM, D_MODEL), lambda b, i: (b, i, 0)),
            pl.BlockSpec((None, None, 6, D_MODEL), lambda b, i: _mod_index(b, i + seg_off)),
            pl.BlockSpec((1, D_MODEL), lambda b, i: (0, 0)),
            pl.BlockSpec((D_MODEL, D_FF), lambda b, i: (0, 0)),
            pl.BlockSpec((D_MODEL, D_FF), lambda b, i: (0, 0)),
            pl.BlockSpec((D_FF, D_MODEL), lambda b, i: (0, 0)),
            pl.BlockSpec((1, D_MODEL), lambda b, i: (0, 0)),
        ],
        out_specs=pl.BlockSpec((None, TM, D_MODEL), lambda b, i: (b, i, 0)),
        out_shape=jax.ShapeDtypeStruct((B, T, D_MODEL), F32),
        compiler_params=_cparams(("arbitrary", "arbitrary")),
        name="ffn",
    )(h, mod, nw, wg, wu, wd, fw)


def _reorder_w_in(w):
    parts = [w[:, 0:1536], w[:, 1536:2048], w[:, 2064:2576], w[:, 2576:3088], w[:, 3088:4624], w[:, 4624:7696],
             w[:, 2048:2064], jnp.zeros((w.shape[0], IN_COLS_PAD - 7696), w.dtype)]
    return jnp.concatenate(parts, axis=1).astype(BF16)


def _block_diag(w):
    g, n, _ = w.shape
    return jnp.einsum('gij,gh->gihj', w, jnp.eye(g, dtype=w.dtype)).reshape(g * n, g * n)


def _lru_gate_weights(wa, ba, wi, bi):
    half = LRU_G // 2
    ws, bs = [], []
    for p in range(2):
        grp = slice(p * half, (p + 1) * half)
        ch = slice(p * MIX_W // 2, (p + 1) * MIX_W // 2)
        cols, bcols = [], []
        for d in range(2):
            cols += [_block_diag(wa[d, grp]), _block_diag(wi[d, grp])]
            bcols += [ba[d, ch], bi[d, ch]]
        ws.append(jnp.concatenate(cols, axis=1))
        bs.append(jnp.concatenate(bcols)[None, :])
    return jnp.stack(ws).astype(BF16), jnp.stack(bs)


def _rope_tables(seq):
    rows = seq // GRID_W
    rowp = jnp.repeat(jnp.arange(rows, dtype=F32), GRID_W)
    colp = jnp.tile(jnp.arange(GRID_W, dtype=F32), rows)
    inv = ROPE_BASE ** (-jnp.arange(ROPE_NF, dtype=F32) / ROPE_NF)
    ar, ac = rowp[:, None] * inv, colp[:, None] * inv
    cs = jnp.concatenate([jnp.cos(ar), jnp.cos(ar), jnp.cos(ac), jnp.cos(ac)], axis=1)
    sn = jnp.concatenate([-jnp.sin(ar), jnp.sin(ar), -jnp.sin(ac), jnp.sin(ac)], axis=1)
    cs = jnp.concatenate([jnp.ones((CTX_LEN, 64), F32), cs], axis=0)
    sn = jnp.concatenate([jnp.zeros((CTX_LEN, 64), F32), sn], axis=0)
    return jnp.tile(cs, (1, 2)), jnp.tile(sn, (1, 2))


def kernel(x, c, ctx, c_ctx, w_mod, b_mod, norm_mix, norm_ffn, w_in, dn_conv, dn_a_log, dn_dt_bias, dn_norm, lru_conv_w, lru_conv_b, lru_wa, lru_ba, lru_wi, lru_bi, lru_lambda, da_lambda, da_norm, w_branch, w_out, w_ffn_gate, w_ffn_up, w_ffn_down, norm_final):
    B, S, D = x.shape
    depth = w_mod.shape[0]
    rows = jnp.concatenate([c, c_ctx[None, :], jnp.zeros((16 - B - 1, D), F32)], axis=0)
    mod = _mod_call(rows, w_mod, b_mod).reshape(depth, 16, 6, D)
    cos_t, sin_t = _rope_tables(S)
    h = jnp.concatenate([ctx, x], axis=1)
    for l in range(depth):
        last = l == depth - 1
        t_off = 1 if last else 0
        lam_init = 0.8 - 0.6 * math.exp(-0.3 * l)
        mod_l = jnp.stack([jnp.broadcast_to(mod[l, B][None], (B, 6, D)), mod[l, :B]], axis=1)
        dn_qkv, dn_z, lru_x, lru_y, da_qkv, gates, ab = _inproj_call(
            h, mod_l, norm_mix[l][None, :], _reorder_w_in(w_in[l]))
        dn_f, dn_b = _dn_seq_call(_dn_chunk_call(dn_qkv, ab, dn_conv[l], dn_a_log[l], dn_dt_bias[l]))
        wg, bg = _lru_gate_weights(lru_wa[l], lru_ba[l], lru_wi[l], lru_bi[l])
        yb = _lru_call(lru_x, lru_y, lru_conv_w[l], lru_conv_b[l][None, :], wg, bg, lru_lambda[l])
        yc_ctx, yc_lat = _attn_call(da_qkv, cos_t, sin_t, da_lambda[l], da_norm[l][None, :], lam_init, not last)
        h = _merge_call(h, mod_l, dn_f, dn_b, dn_z, yb, yc_ctx, yc_lat, gates, dn_norm[l][None, :],
                        w_branch[l].astype(BF16), w_out[l].astype(BF16), t_off)
        h = _ffn_call(h, mod_l, norm_ffn[l][None, :], w_ffn_gate[l].astype(BF16), w_ffn_up[l].astype(BF16),
                      w_ffn_down[l].astype(BF16), norm_final[None, :], t_off, last)
    return h
```

```python
import functools
import math

import jax
import jax.numpy as jnp
import numpy as np
from jax import lax
from jax.experimental import pallas as pl
from jax.experimental.pallas import tpu as pltpu

F32 = jnp.float32
BF16 = jnp.bfloat16

D_MODEL = 1024
CTX_LEN = 256
GRID_W = 64
MIX_W = 512
N_HEAD = 4
HEAD_W = 128
DN_CHUNK = 64
LRU_C = 8.0
LRU_G = 8
ROPE_BASE = 10000.0
ROPE_NF = 16
D_FF = 2816
TM = 256
FF_CHUNK = 512
VT_ROWS = HEAD_W + 16
ATTN_HP = 2
DN_SEQ_BB = 2
VMEM_LIMIT = 56 * 1024 * 1024

C_DNQKV, C_DNZ, C_LRUX, C_LRUY, C_DAQKV, C_GATE, C_AB = 0, 1536, 2048, 2560, 3072, 4608, 7680
IN_COLS_PAD = 7808


def _cparams(sem):
    return pltpu.CompilerParams(dimension_semantics=sem, vmem_limit_bytes=VMEM_LIMIT)


def _sigmoid(x):
    return 1.0 / (1.0 + jnp.exp(-x))


def _softplus(x):
    return jnp.maximum(x, 0.0) + jnp.log(1.0 + jnp.exp(-jnp.abs(x)))


def _dot(a, b):
    return jnp.dot(a, b, preferred_element_type=F32)


def _dot_nt(a, b):
    return lax.dot_general(a, b, (((1,), (1,)), ((), ())), preferred_element_type=F32)


def _dot_tn(a, b):
    return lax.dot_general(a, b, (((0,), (0,)), ((), ())), preferred_element_type=F32)


def _mod_kernel(s_ref, w_ref, b_ref, o_ref):
    s = s_ref[...]
    s = s * _sigmoid(s)
    o_ref[...] = jnp.dot(s, w_ref[...], preferred_element_type=F32,
                         precision=lax.Precision.HIGHEST) + b_ref[...]


def _mod_call(rows, w_mod, b_mod):
    depth = w_mod.shape[0]
    n6 = w_mod.shape[2]
    tn = 1024
    return pl.pallas_call(
        _mod_kernel,
        grid=(depth, n6 // tn),
        in_specs=[
            pl.BlockSpec((16, D_MODEL), lambda l, j: (0, 0)),
            pl.BlockSpec((None, D_MODEL, tn), lambda l, j: (l, 0, j)),
            pl.BlockSpec((None, 1, tn), lambda l, j: (l, 0, j)),
        ],
        out_specs=pl.BlockSpec((None, 16, tn), lambda l, j: (l, 0, j)),
        out_shape=jax.ShapeDtypeStruct((depth, 16, n6), F32),
        compiler_params=_cparams(("arbitrary", "arbitrary")),
        name="adaln_mod",
    )(rows, w_mod, b_mod.reshape(depth, 1, n6))


def _mod_index(b, i):
    return (b, jnp.minimum(i, 1), 0, 0)


def _norm_modulate(x, nw, shift, scale, eps=1e-6):
    ms = jnp.mean(x * x, axis=-1, keepdims=True)
    y = x * lax.rsqrt(ms + eps) * nw
    return y * (1.0 + scale) + shift


_INPROJ_SEGS = (
    (C_DNQKV, 1536), (C_DNZ, 512), (C_LRUX, 512), (C_LRUY, 512), (C_DAQKV, 1536), (C_GATE, 3072), (C_AB, 128))


def _inproj_kernel(h_ref, mod_ref, nw_ref, w_ref, *out_refs):
    u = _norm_modulate(h_ref[...], nw_ref[...], mod_ref[0:1, :], mod_ref[1:2, :]).astype(BF16)
    for (c0, width), o_ref in zip(_INPROJ_SEGS, out_refs):
        step = min(width, 512)
        for c in range(0, width, step):
            o_ref[:, c:c + step] = _dot(u, w_ref[:, c0 + c:c0 + c + step]).astype(o_ref.dtype)


def _inproj_call(h, mod, nw, w):
    B, T, _ = h.shape
    nt = T // TM
    dts = (BF16, BF16, BF16, BF16, BF16, BF16, F32)
    return pl.pallas_call(
        _inproj_kernel,
        grid=(B, nt),
        in_specs=[
            pl.BlockSpec((None, TM, D_MODEL), lambda b, i: (b, i, 0)),
            pl.BlockSpec((None, None, 6, D_MODEL), _mod_index),
            pl.BlockSpec((1, D_MODEL), lambda b, i: (0, 0)),
            pl.BlockSpec((D_MODEL, IN_COLS_PAD), lambda b, i: (0, 0)),
        ],
        out_specs=[pl.BlockSpec((None, TM, wd), lambda b, i: (b, i, 0)) for _, wd in _INPROJ_SEGS],
        out_shape=[jax.ShapeDtypeStruct((B, T, wd), dt) for (_, wd), dt in zip(_INPROJ_SEGS, dts)],
        compiler_params=_cparams(("arbitrary", "arbitrary")),
        name="in_proj",
    )(h, mod, nw, w)


def _halo_valid(i, nt):
    vp = jnp.where(i >= 2, 1.0, 0.0).astype(F32)
    vn = jnp.where(jnp.logical_and(i >= 1, i <= nt - 2), 1.0, 0.0).astype(F32)
    return vp, vn


def _dn_constants():
    r = np.arange(TM)[:, None]
    c = np.arange(TM)[None, :]
    same = (r // DN_CHUNK) == (c // DN_CHUNK)
    shifts = np.stack([c == r - 1, c == r + 1, c == r + 2]).astype(np.float32)
    sums = np.concatenate([same & (c <= r), same], axis=0).astype(np.float32)
    masks = np.stack([same & (c <= r), same & (c >= r), same & (c < r), same & (c > r),
                      (r // 16) == (c // 16), r == c]).astype(np.float32)
    return jnp.asarray(shifts, BF16), jnp.asarray(sums, BF16), jnp.asarray(masks, F32)


def _dn_chunk_kernel(x_ref, xp_ref, xn_ref, ab_ref, cw_ref, al_ref, dtb_ref, sh_ref, sm_ref, mk_ref,
                     wf_ref, uf_ref, qf_ref, kf_ref, qkf_ref, wb_ref, ub_ref, qb_ref, kb_ref, qkb_ref, ge_ref,
                     s_ref, gc_ref, bt_ref, gct_ref, gt_ref, *, nt):
    i = pl.program_id(1)
    vp, vn = _halo_valid(i, nt)
    row8 = lax.broadcasted_iota(jnp.int32, (8, HEAD_W), 0)
    for cb2 in range(3 * N_HEAD // 2):
        c2 = slice(cb2 * 2 * HEAD_W, (cb2 + 1) * 2 * HEAD_W)
        xb = x_ref[:, c2]
        xm1, xp1, xp2 = _dot(sh_ref[0], xb), _dot(sh_ref[1], xb), _dot(sh_ref[2], xb)
        xf = xb.astype(F32)
        for half in range(2):
            cb = cb2 * 2 + half
            cols = slice(cb * HEAD_W, (cb + 1) * HEAD_W)
            hc = slice(half * HEAD_W, (half + 1) * HEAD_W)
            w = cw_ref[:, cols]
            y = w[0:1, :] * xm1[:, hc] + w[1:2, :] * xf[:, hc] + w[2:3, :] * xp1[:, hc] + w[3:4, :] * xp2[:, hc]
            prev = xp_ref[:, cols].astype(F32)[15:16, :] * vp
            nxt = xn_ref[:, cols].astype(F32)
            n0, n1 = nxt[0:1, :] * vn, nxt[1:2, :] * vn
            top = jnp.where(row8 == 0, w[0:1, :] * prev, 0.0)
            bot = jnp.where(row8 == 6, w[3:4, :] * n0, jnp.where(row8 == 7, w[2:3, :] * n0 + w[3:4, :] * n1, 0.0))
            y = jnp.concatenate([y[0:8, :] + top, y[8:TM - 8, :], y[TM - 8:TM, :] + bot], axis=0)
            s = y * _sigmoid(y)
            if cb < 2 * N_HEAD:
                ss = jnp.sum(s * s, axis=-1, keepdims=True)
                s = s * (lax.rsqrt(ss + 1e-6) * (HEAD_W ** -0.5 if cb < N_HEAD else 1.0))
            s_ref[cb] = s

    ab = ab_ref[...]
    lane = lax.broadcasted_iota(jnp.int32, (TM, 128), 1)
    g = -jnp.exp(al_ref[...]) * _softplus(ab + dtb_ref[...])
    g1 = g.astype(BF16)
    r1 = g - g1.astype(F32)
    g2 = r1.astype(BF16)
    g3 = (r1 - g2.astype(F32)).astype(BF16)
    sums = _dot(sm_ref[...], jnp.concatenate([g1, g2, g3], axis=1))
    sums = sums[:, 0:128] + sums[:, 128:256] + sums[:, 256:384]
    g_f = sums[0:TM, :]
    gtot = sums[TM:2 * TM, :]
    gc = jnp.where(lane < N_HEAD, g_f, gtot - g_f + g)
    gc_ref[...] = gc
    bt_ref[...] = _sigmoid(ab)
    gt_ref[...] = gtot
    gct_ref[...] = gc.T
    ge_ref[...] = jnp.exp(gtot)
    out_refs = ((wf_ref, uf_ref, qf_ref, kf_ref, qkf_ref), (wb_ref, ub_ref, qb_ref, kb_ref, qkb_ref))
    sub8 = lax.broadcasted_iota(jnp.int32, (8, TM), 0)

    probs = [(h, d) for h in range(N_HEAD) for d in range(2)]
    kk, qk0 = {}, {}
    for h in range(N_HEAD):
        kb16 = s_ref[N_HEAD + h].astype(BF16)
        kk[h] = _dot_nt(kb16, kb16)
        qk0[h] = _dot_nt(s_ref[h].astype(BF16), kb16)
    tmat, dblk, lo_b, rhs_b = {}, {}, {}, {}
    for h, d in probs:
        q = s_ref[h]
        k = s_ref[N_HEAD + h]
        v = s_ref[2 * N_HEAD + h]
        idx = d * N_HEAD + h
        sel = lane == idx
        gc_col = jnp.sum(jnp.where(sel, gc_ref[...], 0.0), axis=1, keepdims=True)
        gt_col = jnp.sum(jnp.where(sel, gt_ref[...], 0.0), axis=1, keepdims=True)
        bt_col = jnp.sum(jnp.where(lane == idx + 2 * N_HEAD, bt_ref[...], 0.0), axis=1, keepdims=True)
        gc_row = jnp.sum(jnp.where(sub8 == idx, gct_ref[0:8, :], 0.0), axis=0, keepdims=True)
        e = jnp.exp(jnp.minimum(gc_col - gc_row, 0.0))
        a = kk[h] * (e * mk_ref[2 + d] * bt_col)
        eg = jnp.exp(gc_col)
        _, _, q_r, k_r, qk_r = out_refs[d]
        q_r[h] = (q * eg).astype(BF16)
        k_r[h] = (k * jnp.exp(gt_col - gc_col)).astype(BF16)
        qk_r[h] = (qk0[h] * (e * mk_ref[d])).astype(BF16)
        dm = a * mk_ref[4]
        tmat[h, d] = mk_ref[5] - dm
        lo_b[h, d] = (a - dm).astype(BF16)
        rhs_b[h, d] = jnp.concatenate([k * (bt_col * eg), v * bt_col], axis=1).astype(BF16)
        dblk[h, d] = dm.astype(BF16)
    pw = {pk: _dot(dblk[pk], dblk[pk]) for pk in probs}
    for m in range(3):
        for pk in probs:
            pb = pw[pk].astype(BF16)
            tmat[pk] = tmat[pk] + _dot(tmat[pk].astype(BF16), pb)
            if m < 2:
                pw[pk] = _dot(pb, pb)
    x0, mt, m2 = {}, {}, {}
    for pk in probs:
        tb = tmat[pk].astype(BF16)
        x0[pk] = _dot(tb, rhs_b[pk])
        mt[pk] = _dot(tb, lo_b[pk]).astype(BF16)
    for pk in probs:
        m2[pk] = _dot(mt[pk], mt[pk]).astype(BF16)
    ys = {pk: x0[pk] + _dot(m2[pk], x0[pk].astype(BF16)) for pk in probs}
    for h, d in probs:
        x = ys[h, d] - _dot(mt[h, d], ys[h, d].astype(BF16))
        w_r, u_r = out_refs[d][0], out_refs[d][1]
        w_r[h] = x[:, :HEAD_W].astype(BF16)
        u_r[h] = x[:, HEAD_W:]


def _dn_chunk_call(qkv, ab, conv_w, a_log, dt_bias):
    B, T, W = qkv.shape
    nt = T // TM
    hb = TM // 16
    last16 = T // 16 - 1
    al = jnp.zeros((1, 128), F32).at[0, :2 * N_HEAD].set(a_log.reshape(-1))
    dtb = jnp.zeros((1, 128), F32).at[0, :2 * N_HEAD].set(dt_bias.reshape(-1))
    shifts, sums, masks = _dn_constants()
    hspec = lambda w: pl.BlockSpec((None, N_HEAD, TM, w), lambda b, i: (b, 0, i, 0))
    hshape = lambda w, dt: jax.ShapeDtypeStruct((B, N_HEAD, T, w), dt)
    const = lambda a: pl.BlockSpec(a.shape, lambda b, i: (0,) * a.ndim)
    per_dir_specs = [hspec(HEAD_W), hspec(HEAD_W), hspec(HEAD_W), hspec(HEAD_W), hspec(TM)]
    per_dir_shapes = [hshape(HEAD_W, BF16), hshape(HEAD_W, F32), hshape(HEAD_W, BF16), hshape(HEAD_W, BF16),
                      hshape(TM, BF16)]
    return pl.pallas_call(
        functools.partial(_dn_chunk_kernel, nt=nt),
        grid=(B, nt),
        in_specs=[
            pl.BlockSpec((None, TM, W), lambda b, i: (b, i, 0)),
            pl.BlockSpec((None, 16, W), lambda b, i: (b, jnp.maximum(i * hb - 1, 0), 0)),
            pl.BlockSpec((None, 16, W), lambda b, i: (b, jnp.minimum((i + 1) * hb, last16), 0)),
            pl.BlockSpec((None, TM, 128), lambda b, i: (b, i, 0)),
            pl.BlockSpec((4, W), lambda b, i: (0, 0)),
            pl.BlockSpec((1, 128), lambda b, i: (0, 0)),
            pl.BlockSpec((1, 128), lambda b, i: (0, 0)),
            const(shifts), const(sums), const(masks),
        ],
        out_specs=per_dir_specs + per_dir_specs + [pl.BlockSpec((None, TM, 128), lambda b, i: (b, i, 0))],
        out_shape=per_dir_shapes + per_dir_shapes + [jax.ShapeDtypeStruct((B, T, 128), F32)],
        scratch_shapes=[
            pltpu.VMEM((3 * N_HEAD, TM, HEAD_W), F32),
            pltpu.VMEM((TM, 128), F32), pltpu.VMEM((TM, 128), F32),
            pltpu.VMEM((128, TM), F32), pltpu.VMEM((TM, 128), F32),
        ],
        compiler_params=_cparams(("arbitrary", "arbitrary")),
        name="dn_chunk",
    )(qkv, qkv, qkv, ab, conv_w, al, dtb, shifts, sums, masks)


def _dn_seq_kernel(wf_ref, uf_ref, qf_ref, kf_ref, qkf_ref, gef_ref, wb_ref, ub_ref, qb_ref, kb_ref, qkb_ref,
                   geb_ref, of_ref, ob_ref, st_ref):
    C = DN_CHUNK
    nc = TM // C

    @pl.when(pl.program_id(1) == 0)
    def _():
        st_ref[...] = jnp.zeros_like(st_ref)

    ins = ((wf_ref, uf_ref, qf_ref, kf_ref, qkf_ref, gef_ref, of_ref),
           (wb_ref, ub_ref, qb_ref, kb_ref, qkb_ref, geb_ref, ob_ref))
    chains = [(d, bb, h) for d in range(2) for bb in range(DN_SEQ_BB) for h in range(N_HEAD)]
    for ci in range(nc):
        st, sb, ws, qs, vb = {}, {}, {}, {}, {}
        for ch in chains:
            d, bb, h = ch
            c = ci if d == 0 else nc - 1 - ci
            rows = slice(c * C, (c + 1) * C)
            st[ch] = st_ref[(d * DN_SEQ_BB + bb) * N_HEAD + h]
            sb[ch] = st[ch].astype(BF16)
            ws[ch] = _dot(ins[d][0][bb, h, rows, :], sb[ch])
            qs[ch] = _dot(ins[d][2][bb, h, rows, :], sb[ch])
        for ch in chains:
            d, bb, h = ch
            c = ci if d == 0 else nc - 1 - ci
            rows = slice(c * C, (c + 1) * C)
            vb[ch] = (ins[d][1][bb, h, rows, :] - ws[ch]).astype(BF16)
        for ch in chains:
            d, bb, h = ch
            c = ci if d == 0 else nc - 1 - ci
            rows = slice(c * C, (c + 1) * C)
            _, _, _, k_r, qk_r, ge_r, o_r = ins[d]
            si = d * N_HEAD + h
            ge = ge_r[bb, c * C:c * C + 8, :][0:1, si:si + 1]
            o_r[bb, rows, h * HEAD_W:(h + 1) * HEAD_W] = qs[ch] + _dot(qk_r[bb, h, rows, c * C:(c + 1) * C], vb[ch])
            st_ref[(d * DN_SEQ_BB + bb) * N_HEAD + h] = ge * st[ch] + _dot_tn(k_r[bb, h, rows, :], vb[ch])


def _dn_seq_call(chunk_outs):
    wf, uf, qf, kf, qkf, wb, ub, qb, kb, qkb, ge = chunk_outs
    B, _, T, _ = wf.shape
    nt = T // TM
    bb = DN_SEQ_BB
    tf = lambda s: s
    tb = lambda s: jnp.where(s == 0, 0, nt - s)
    hspec = lambda w, t: pl.BlockSpec((bb, N_HEAD, TM, w), lambda b, s: (b, 0, t(s), 0))
    gspec = lambda t: pl.BlockSpec((bb, TM, 128), lambda b, s: (b, t(s), 0))
    dir_specs = lambda t: [hspec(HEAD_W, t), hspec(HEAD_W, t), hspec(HEAD_W, t), hspec(HEAD_W, t), hspec(TM, t),
                           gspec(t)]
    ospec = lambda t: pl.BlockSpec((bb, TM, MIX_W), lambda b, s: (b, t(s), 0))
    return pl.pallas_call(
        _dn_seq_kernel,
        grid=(B // bb, nt),
        in_specs=dir_specs(tf) + dir_specs(tb),
        out_specs=[ospec(tf), ospec(tb)],
        out_shape=[jax.ShapeDtypeStruct((B, T, MIX_W), F32)] * 2,
        scratch_shapes=[pltpu.VMEM((2 * bb * N_HEAD, HEAD_W, HEAD_W), F32)],
        compiler_params=_cparams(("arbitrary", "arbitrary")),
        name="dn_seq",
    )(wf, uf, qf, kf, qkf, ge, wb, ub, qb, kb, qkb, ge)


def _gelu_tanh(x):
    return 0.5 * x * (1.0 + jnp.tanh(math.sqrt(2.0 / math.pi) * (x + 0.044715 * (x * x * x))))


def _lru_kernel(x_ref, y_ref, cw_ref, cb_ref, wg_ref, bg_ref, lam_ref, sh_ref, o_ref, hf_ref, xc_ref, *, nt):
    CW = x_ref.shape[1]
    T = x_ref.shape[0]
    G = TM // 8
    row8 = lax.broadcasted_iota(jnp.int32, (8, CW), 0)
    sub = lax.broadcasted_iota(jnp.int32, (G, 8, CW), 1)

    def conv(t):
        base = pl.multiple_of(t * TM, TM)
        xb = x_ref[pl.ds(base, TM), :]
        xp = x_ref[pl.ds(pl.multiple_of(jnp.maximum(base - 16, 0), 16), 16), :].astype(F32)
        xn = x_ref[pl.ds(pl.multiple_of(jnp.minimum(base + TM, T - 16), 16), 16), :].astype(F32)
        vp, vn = _halo_valid(t, nt)
        w = cw_ref[...]
        y = (w[0:1, :] * _dot(sh_ref[0], xb) + w[1:2, :] * xb.astype(F32)
             + w[2:3, :] * _dot(sh_ref[1], xb) + w[3:4, :] * _dot(sh_ref[2], xb))
        n0, n1 = xn[0:1, :] * vn, xn[1:2, :] * vn
        top = jnp.where(row8 == 0, w[0:1, :] * (xp[15:16, :] * vp), 0.0)
        bot = jnp.where(row8 == 6, w[3:4, :] * n0, jnp.where(row8 == 7, w[2:3, :] * n0 + w[3:4, :] * n1, 0.0))
        return jnp.concatenate([y[0:8, :] + top, y[8:TM - 8, :], y[TM - 8:TM, :] + bot], axis=0) + cb_ref[...]

    def coeffs(xc, d):
        z = _dot(xc.astype(BF16), wg_ref[:, d * 2 * CW:(d + 1) * 2 * CW]) + bg_ref[:, d * 2 * CW:(d + 1) * 2 * CW]
        rg = _sigmoid(z[:, :CW])
        ig = _sigmoid(z[:, CW:])
        log_a = (-LRU_C * _softplus(-lam_ref[d:d + 1, :])) * rg
        a = jnp.exp(log_a)
        om = 1.0 - a * a
        b = jnp.where(om > 0.0, om * lax.rsqrt(om), 0.0) * (ig * xc)
        return a, b

    def scan_tile(a, b, h_in, d):
        a = a.reshape(G, 8, CW)
        b = b.reshape(G, 8, CW)
        for s in (1, 2, 4):
            if d == 0:
                a_s, b_s, valid = pltpu.roll(a, s, 1), pltpu.roll(b, s, 1), sub >= s
            else:
                a_s, b_s, valid = pltpu.roll(a, 8 - s, 1), pltpu.roll(b, 8 - s, 1), sub < 8 - s
            b = jnp.where(valid, a * b_s + b, b)
            a = jnp.where(valid, a * a_s, a)
        hs = [None] * G
        h = h_in
        for g in (range(G) if d == 0 else range(G - 1, -1, -1)):
            hg = b[g] + a[g] * h
            hs[g] = hg
            h = hg[7:8, :] if d == 0 else hg[0:1, :]
        return jnp.concatenate(hs, axis=0), h

    def fwd(t, h):
        rows = pl.ds(pl.multiple_of(t * TM, TM), TM)
        xc = conv(t)
        xc_ref[rows, :] = xc
        a, b = coeffs(xc, 0)
        hf, h = scan_tile(a, b, h, 0)
        hf_ref[rows, :] = hf
        return h

    lax.fori_loop(0, nt, fwd, jnp.zeros((1, CW), F32))

    def bwd(s, h):
        t = jnp.where(s == 0, 0, nt - s)
        rows = pl.ds(pl.multiple_of(t * TM, TM), TM)
        a, b = coeffs(xc_ref[rows, :], 1)
        hb, h = scan_tile(a, b, h, 1)
        y = y_ref[rows, :].astype(F32)
        o_ref[rows, :] = ((hf_ref[rows, :] + hb) * _gelu_tanh(y)).astype(o_ref.dtype)
        return h

    lax.fori_loop(0, nt, bwd, jnp.zeros((1, CW), F32))


def _lru_call(x, y, conv_w, conv_b, wg, bg, lam):
    B, T, W = x.shape
    cw = W // 2
    nt = T // TM
    shifts = _dn_constants()[0]
    return pl.pallas_call(
        functools.partial(_lru_kernel, nt=nt),
        grid=(B, 2),
        in_specs=[
            pl.BlockSpec((None, T, cw), lambda b, p: (b, 0, p)),
            pl.BlockSpec((None, T, cw), lambda b, p: (b, 0, p)),
            pl.BlockSpec((4, cw), lambda b, p: (0, p)),
            pl.BlockSpec((1, cw), lambda b, p: (0, p)),
            pl.BlockSpec((None, cw, 4 * cw), lambda b, p: (p, 0, 0)),
            pl.BlockSpec((None, 1, 4 * cw), lambda b, p: (p, 0, 0)),
            pl.BlockSpec((2, cw), lambda b, p: (0, p)),
            pl.BlockSpec(shifts.shape, lambda b, p: (0, 0, 0)),
        ],
        out_specs=pl.BlockSpec((None, T, cw), lambda b, p: (b, 0, p)),
        out_shape=jax.ShapeDtypeStruct((B, T, W), BF16),
        scratch_shapes=[pltpu.VMEM((T, cw), F32), pltpu.VMEM((T, cw), F32)],
        compiler_params=_cparams(("arbitrary", "arbitrary")),
        name="rglru",
    )(x, y, conv_w, conv_b, wg, bg, lam, shifts)


def _rope(x, cs, sn, first_half):
    swapped = jnp.where(first_half, pltpu.roll(x, 112, 1), pltpu.roll(x, 16, 1))
    return x * cs + swapped * sn


def _attn_kernel(*refs, lam_init, n_q, nt):
    q_refs = refs[:n_q]
    k_ref, v_ref = refs[n_q:n_q + 2]
    tab_refs = refs[n_q + 2:3 * n_q + 2]
    ck_ref, sk_ref, lv_ref, nw_ref, o_ref, kr_ref, vt_ref, qt_ref, st0_ref, st1_ref, acc_ref = refs[3 * n_q + 2:]
    lane = lax.broadcasted_iota(jnp.int32, (TM, HEAD_W), 1)
    first_half = (lane % 32) < 16
    heads = [(hh, slice(hh * HEAD_W, (hh + 1) * HEAD_W)) for hh in range(ATTN_HP)]
    chains = [(qi, hh) for qi in range(n_q) for hh in range(ATTN_HP)]

    @pl.when(pl.program_id(2) == 0)
    def _():
        extra = (lax.broadcasted_iota(jnp.int32, (VT_ROWS - HEAD_W, TM), 0) == 0).astype(BF16)

        def prep_kv(t, carry):
            rows = pl.ds(pl.multiple_of(t * TM, TM), TM)
            for hh, cols in heads:
                kr_ref[hh, rows, :] = _rope(k_ref[rows, cols].astype(F32), ck_ref[rows, :], sk_ref[rows, :],
                                            first_half).astype(BF16)
                vt_ref[hh, 0:HEAD_W, rows] = v_ref[rows, cols].astype(F32).T.astype(BF16)
                vt_ref[hh, HEAD_W:VT_ROWS, rows] = extra
            return carry
        lax.fori_loop(0, nt, prep_kv, 0)

    for qi, hh in chains:
        q = _rope(q_refs[qi][:, heads[hh][1]].astype(F32), tab_refs[2 * qi][...], tab_refs[2 * qi + 1][...],
                  first_half) * (64 ** -0.5 * math.log2(math.e))
        qt_ref[qi * ATTN_HP + hh] = jnp.concatenate(
            [jnp.where(lane < 64, q, 0.0).T, jnp.where(lane >= 64, q, 0.0).T], axis=1).astype(BF16)
        acc_ref[qi * ATTN_HP + hh] = jnp.zeros((VT_ROWS, 2 * TM), F32)

    def scores(j, st_ref):
        mx = {}
        for qi, hh in chains:
            ci = qi * ATTN_HP + hh
            st = _dot(kr_ref[hh, j * TM:(j + 1) * TM, :], qt_ref[ci])
            st_ref[ci] = st
            mx[ci] = jnp.max(st, axis=0, keepdims=True)
        return mx

    def softmax_pv(j, st_ref, ms, mx):
        out = {}
        for qi, hh in chains:
            ci = qi * ATTN_HP + hh
            m_new = jnp.maximum(ms[ci], mx[ci])
            p = jnp.exp2(st_ref[ci] - m_new).astype(BF16)
            acc_ref[ci] = (jnp.exp2(ms[ci] - m_new) * acc_ref[ci]
                           + _dot(vt_ref[hh, :, j * TM:(j + 1) * TM], p))
            out[ci] = m_new
        return out

    bufs = (st0_ref, st1_ref)
    mx = scores(0, st0_ref)
    ms = {qi * ATTN_HP + hh: jnp.full((1, 2 * TM), -1e30, F32) for qi, hh in chains}
    for j in range(nt):
        mx_next = scores(j + 1, bufs[(j + 1) % 2]) if j + 1 < nt else None
        ms = softmax_pv(j, bufs[j % 2], ms, mx)
        mx = mx_next
    lv = lv_ref[...]
    lam = (jnp.exp(jnp.sum(lv[0:1, :] * lv[1:2, :], axis=-1, keepdims=True))
           - jnp.exp(jnp.sum(lv[2:3, :] * lv[3:4, :], axis=-1, keepdims=True)) + lam_init)
    for qi, hh in chains:
        ci = qi * ATTN_HP + hh
        on = acc_ref[ci, 0:HEAD_W, :] * (1.0 / acc_ref[ci, HEAD_W:HEAD_W + 1, :])
        o = (on[:, :TM] - lam * on[:, TM:]).T
        ms_o = jnp.mean(o * o, axis=-1, keepdims=True)
        o_ref[qi * TM:(qi + 1) * TM, heads[hh][1]] = (
            o * lax.rsqrt(ms_o + 1e-5) * nw_ref[...] * (1.0 - lam_init)).astype(o_ref.dtype)


def _attn_call(qkv, cos_t, sin_t, lam_vecs, sub_norm, lam_init, ctx):
    B, T, _ = qkv.shape
    pw = ATTN_HP * HEAD_W
    npair = N_HEAD // ATTN_HP
    if ctx:
        n_q, nk, nsteps, q_tiles = 1, 1, 1, [lambda s: 0]
    else:
        assert (T // TM - 1) % 2 == 0, "latent query tiles are processed in pairs"
        n_q, nk, nsteps = 2, T // TM, (T // TM - 1) // 2
        q_tiles = [lambda s: 2 * s + 1, lambda s: 2 * s + 2]
    nch = n_q * ATTN_HP
    qspec = lambda t: pl.BlockSpec((None, TM, pw), lambda b, h, s: (b, t(s), h))
    tspec = lambda t: pl.BlockSpec((TM, HEAD_W), lambda b, h, s: (t(s), 0))
    tabs = [tspec(t) for t in q_tiles for _ in range(2)]
    kern = functools.partial(_attn_kernel, lam_init=lam_init, n_q=n_q, nt=nk)
    return pl.pallas_call(
        kern,
        grid=(B, npair, nsteps),
        in_specs=[qspec(t) for t in q_tiles] + [
            pl.BlockSpec((None, nk * TM, pw), lambda b, h, s: (b, 0, npair + h)),
            pl.BlockSpec((None, nk * TM, pw), lambda b, h, s: (b, 0, 2 * npair + h)),
        ] + tabs + [
            pl.BlockSpec((nk * TM, HEAD_W), lambda b, h, s: (0, 0)),
            pl.BlockSpec((nk * TM, HEAD_W), lambda b, h, s: (0, 0)),
            pl.BlockSpec((4, 64), lambda b, h, s: (0, 0)),
            pl.BlockSpec((1, HEAD_W), lambda b, h, s: (0, 0)),
        ],
        out_specs=pl.BlockSpec((None, n_q * TM, pw), lambda b, h, s: (b, s, h)),
        out_shape=jax.ShapeDtypeStruct((B, nsteps * n_q * TM, MIX_W), BF16),
        scratch_shapes=[pltpu.VMEM((ATTN_HP, nk * TM, HEAD_W), BF16), pltpu.VMEM((ATTN_HP, VT_ROWS, nk * TM), BF16),
                        pltpu.VMEM((nch, HEAD_W, 2 * TM), BF16),
                        pltpu.VMEM((nch, TM, 2 * TM), F32), pltpu.VMEM((nch, TM, 2 * TM), F32),
                        pltpu.VMEM((nch, VT_ROWS, 2 * TM), F32)],
        compiler_params=_cparams(("arbitrary", "arbitrary", "arbitrary")),
        name="diff_attn_ctx" if ctx else "diff_attn",
    )(*([qkv] * (n_q + 2)), *([cos_t, sin_t] * n_q), cos_t, sin_t, lam_vecs, sub_norm)


def _mix_ffn_kernel(h_ref, mod_ref, dnf_ref, dnb_ref, z_ref, yb_ref, *rest, t_off, final):
    (ycl_ref, g_ref, dnw_ref, wb_ref, wo_ref, nw_ref, wg_ref, wu_ref, wd_ref, fw_ref, o_ref) = rest[-11:]
    yc = ycl_ref[...]
    if len(rest) == 12:
        yc = jnp.where(pl.program_id(1) + t_off == 0, rest[0][...], yc)
    dnw = dnw_ref[...]
    ya_parts = []
    for hh in range(N_HEAD):
        cols = slice(hh * HEAD_W, (hh + 1) * HEAD_W)
        o = dnf_ref[:, cols] + dnb_ref[:, cols]
        ms = jnp.mean(o * o, axis=-1, keepdims=True)
        z = z_ref[:, cols].astype(F32)
        ya_parts.append(((o * lax.rsqrt(ms + 1e-6) * dnw) * (z * _sigmoid(z))).astype(BF16))
    ys = (jnp.concatenate(ya_parts, axis=1), yb_ref[...], yc)
    mix = None
    for n in range(3):
        up = _dot(ys[n], wb_ref[n])
        term = _sigmoid(g_ref[:, n * D_MODEL:(n + 1) * D_MODEL].astype(F32)) * up
        mix = term if mix is None else mix + term
    h = h_ref[...] + mod_ref[2:3, :] * _dot(mix.astype(BF16), wo_ref[...])

    x = _norm_modulate(h, nw_ref[...], mod_ref[3:4, :], mod_ref[4:5, :]).astype(BF16)
    acc = None
    for c in range(0, D_FF, FF_CHUNK):
        ce = min(c + FF_CHUNK, D_FF)
        g = _dot(x, wg_ref[:, c:ce])
        u = _dot(x, wu_ref[:, c:ce])
        part = _dot(((g * _sigmoid(g)) * u).astype(BF16), wd_ref[c:ce, :])
        acc = part if acc is None else acc + part
    out = h + mod_ref[5:6, :] * acc
    if final:
        ms = jnp.mean(out * out, axis=-1, keepdims=True)
        out = out * lax.rsqrt(ms + 1e-6) * fw_ref[...]
    o_ref[...] = out


def _mix_ffn_call(h, mod, dnf, dnb, z, yb, yc_ctx, yc_lat, gates, dn_norm, wb, wo, nw, wg, wu, wd, fw, t_off, final):
    B, T, _ = h.shape
    nt = T // TM - t_off
    row = lambda w: pl.BlockSpec((None, TM, w), lambda b, i: (b, i + t_off, 0))
    const = lambda a: pl.BlockSpec(a.shape, lambda b, i: (0,) * a.ndim, pipeline_mode=pl.Buffered(1))
    yc_specs = [pl.BlockSpec((None, TM, MIX_W), lambda b, i: (b, jnp.maximum(i + t_off - 1, 0), 0))]
    yc_args = [yc_lat]
    if yc_ctx is not None:
        yc_specs.insert(0, pl.BlockSpec((None, TM, MIX_W), lambda b, i: (b, 0, 0)))
        yc_args.insert(0, yc_ctx)
    return pl.pallas_call(
        functools.partial(_mix_ffn_kernel, t_off=t_off, final=final),
        grid=(B, nt),
        in_specs=[
            row(D_MODEL),
            pl.BlockSpec((None, None, 6, D_MODEL), lambda b, i: _mod_index(b, i + t_off)),
            row(MIX_W), row(MIX_W), row(MIX_W), row(MIX_W), *yc_specs, row(3 * D_MODEL),
            const(dn_norm), const(wb), const(wo), const(nw), const(wg), const(wu), const(wd), const(fw),
        ],
        out_specs=pl.BlockSpec((None, TM, D_MODEL), lambda b, i: (b, i, 0)),
        out_shape=jax.ShapeDtypeStruct((B, nt * TM, D_MODEL), F32),
        compiler_params=_cparams(("arbitrary", "arbitrary")),
        name="mix_ffn",
    )(h, mod, dnf, dnb, z, yb, *yc_args, gates, dn_norm, wb, wo, nw, wg, wu, wd, fw)


def _reorder_w_in(w):
    parts = [w[:, 0:1536], w[:, 1536:2048], w[:, 2064:2576], w[:, 2576:3088], w[:, 3088:4624], w[:, 4624:7696],
             w[:, 2048:2064], jnp.zeros((w.shape[0], IN_COLS_PAD - 7696), w.dtype)]
    return jnp.concatenate(parts, axis=1).astype(BF16)


def _block_diag(w):
    g, n, _ = w.shape
    return jnp.einsum('gij,gh->gihj', w, jnp.eye(g, dtype=w.dtype)).reshape(g * n, g * n)


def _lru_gate_weights(wa, ba, wi, bi):
    half = LRU_G // 2
    ws, bs = [], []
    for p in range(2):
        grp = slice(p * half, (p + 1) * half)
        ch = slice(p * MIX_W // 2, (p + 1) * MIX_W // 2)
        cols, bcols = [], []
        for d in range(2):
            cols += [_block_diag(wa[d, grp]), _block_diag(wi[d, grp])]
            bcols += [ba[d, ch], bi[d, ch]]
        ws.append(jnp.concatenate(cols, axis=1))
        bs.append(jnp.concatenate(bcols)[None, :])
    return jnp.stack(ws).astype(BF16), jnp.stack(bs)


def _rope_tables(seq):
    rows = seq // GRID_W
    rowp = jnp.repeat(jnp.arange(rows, dtype=F32), GRID_W)
    colp = jnp.tile(jnp.arange(GRID_W, dtype=F32), rows)
    inv = ROPE_BASE ** (-jnp.arange(ROPE_NF, dtype=F32) / ROPE_NF)
    ar, ac = rowp[:, None] * inv, colp[:, None] * inv
    cs = jnp.concatenate([jnp.cos(ar), jnp.cos(ar), jnp.cos(ac), jnp.cos(ac)], axis=1)
    sn = jnp.concatenate([-jnp.sin(ar), jnp.sin(ar), -jnp.sin(ac), jnp.sin(ac)], axis=1)
    cs = jnp.concatenate([jnp.ones((CTX_LEN, 64), F32), cs], axis=0)
    sn = jnp.concatenate([jnp.zeros((CTX_LEN, 64), F32), sn], axis=0)
    return jnp.tile(cs, (1, 2)), jnp.tile(sn, (1, 2))


def kernel(x, c, ctx, c_ctx, w_mod, b_mod, norm_mix, norm_ffn, w_in, dn_conv, dn_a_log, dn_dt_bias, dn_norm, lru_conv_w, lru_conv_b, lru_wa, lru_ba, lru_wi, lru_bi, lru_lambda, da_lambda, da_norm, w_branch, w_out, w_ffn_gate, w_ffn_up, w_ffn_down, norm_final):
    B, S, D = x.shape
    depth = w_mod.shape[0]
    rows = jnp.concatenate([c, c_ctx[None, :], jnp.zeros((16 - B - 1, D), F32)], axis=0)
    mod = _mod_call(rows, w_mod, b_mod).reshape(depth, 16, 6, D)
    cos_t, sin_t = _rope_tables(S)
    h = jnp.concatenate([ctx, x], axis=1)
    for l in range(depth):
        last = l == depth - 1
        t_off = 1 if last else 0
        lam_init = 0.8 - 0.6 * math.exp(-0.3 * l)
        mod_l = jnp.stack([jnp.broadcast_to(mod[l, B][None], (B, 6, D)), mod[l, :B]], axis=1)
        dn_qkv, dn_z, lru_x, lru_y, da_qkv, gates, ab = _inproj_call(
            h, mod_l, norm_mix[l][None, :], _reorder_w_in(w_in[l]))
        dn_f, dn_b = _dn_seq_call(_dn_chunk_call(dn_qkv, ab, dn_conv[l], dn_a_log[l], dn_dt_bias[l]))
        wg, bg = _lru_gate_weights(lru_wa[l], lru_ba[l], lru_wi[l], lru_bi[l])
        yb = _lru_call(lru_x, lru_y, lru_conv_w[l], lru_conv_b[l][None, :], wg, bg, lru_lambda[l])
        attn = functools.partial(_attn_call, da_qkv, cos_t, sin_t, da_lambda[l], da_norm[l][None, :], lam_init)
        yc_ctx, yc_lat = (None if last else attn(True)), attn(False)
        h = _mix_ffn_call(h, mod_l, dn_f, dn_b, dn_z, yb, yc_ctx, yc_lat, gates, dn_norm[l][None, :],
                          w_branch[l].astype(BF16), w_out[l].astype(BF16), norm_ffn[l][None, :],
                          w_ffn_gate[l].astype(BF16), w_ffn_up[l].astype(BF16), w_ffn_down[l].astype(BF16),
                          norm_final[None, :], t_off, last)
    return h
```

```python
import functools
import math

import jax
import jax.numpy as jnp
import numpy as np
from jax import lax
from jax.experimental import pallas as pl
from jax.experimental.pallas import tpu as pltpu

F32 = jnp.float32
BF16 = jnp.bfloat16

D_MODEL = 1024
CTX_LEN = 256
GRID_W = 64
MIX_W = 512
N_HEAD = 4
HEAD_W = 128
DN_CHUNK = 64
LRU_C = 8.0
LRU_G = 8
ROPE_BASE = 10000.0
ROPE_NF = 16
D_FF = 2816
TM = 256
FF_CHUNK = 512
VT_ROWS = HEAD_W + 16
ATTN_HP = 2
ATTN_KB = 1
DN_SEQ_BB = 2
VMEM_LIMIT = 56 * 1024 * 1024

C_DNQKV, C_DNZ, C_LRUX, C_LRUY, C_DAQKV, C_GATE, C_AB = 0, 1536, 2048, 2560, 3072, 4608, 7680
IN_COLS_PAD = 7808


def _cparams(sem):
    return pltpu.CompilerParams(dimension_semantics=sem, vmem_limit_bytes=VMEM_LIMIT)


def _sigmoid(x):
    return 1.0 / (1.0 + jnp.exp(-x))


def _softplus(x):
    return jnp.maximum(x, 0.0) + jnp.log(1.0 + jnp.exp(-jnp.abs(x)))


def _dot(a, b):
    return jnp.dot(a, b, preferred_element_type=F32)


def _dot_nt(a, b):
    return lax.dot_general(a, b, (((1,), (1,)), ((), ())), preferred_element_type=F32)


def _dot_tn(a, b):
    return lax.dot_general(a, b, (((0,), (0,)), ((), ())), preferred_element_type=F32)


def _mod_kernel(s_ref, w_ref, b_ref, o_ref):
    s = s_ref[...]
    s = s * _sigmoid(s)
    o_ref[...] = jnp.dot(s, w_ref[...], preferred_element_type=F32,
                         precision=lax.Precision.HIGHEST) + b_ref[...]


def _mod_call(rows, w_mod, b_mod):
    depth = w_mod.shape[0]
    n6 = w_mod.shape[2]
    tn = 1024
    return pl.pallas_call(
        _mod_kernel,
        grid=(depth, n6 // tn),
        in_specs=[
            pl.BlockSpec((16, D_MODEL), lambda l, j: (0, 0)),
            pl.BlockSpec((None, D_MODEL, tn), lambda l, j: (l, 0, j)),
            pl.BlockSpec((None, 1, tn), lambda l, j: (l, 0, j)),
        ],
        out_specs=pl.BlockSpec((None, 16, tn), lambda l, j: (l, 0, j)),
        out_shape=jax.ShapeDtypeStruct((depth, 16, n6), F32),
        compiler_params=_cparams(("arbitrary", "arbitrary")),
        name="adaln_mod",
    )(rows, w_mod, b_mod.reshape(depth, 1, n6))


def _mod_index(b, i):
    return (b, jnp.minimum(i, 1), 0, 0)


def _stream_specs(parts, t_off):
    if len(parts) == 1:
        return [pl.BlockSpec((None, TM, D_MODEL), lambda b, i: (b, i + t_off, 0))]
    assert t_off == 0
    return [pl.BlockSpec((None, TM, D_MODEL), lambda b, i: (b, 0, 0)),
            pl.BlockSpec((None, TM, D_MODEL), lambda b, i: (b, jnp.maximum(i - 1, 0), 0))]


def _read_stream(refs):
    if len(refs) == 1:
        return refs[0][...]
    return jnp.where(pl.program_id(1) == 0, refs[0][...], refs[1][...])


def _norm_modulate(x, nw, shift, scale, eps=1e-6):
    ms = jnp.mean(x * x, axis=-1, keepdims=True)
    y = x * lax.rsqrt(ms + eps) * nw
    return y * (1.0 + scale) + shift


_INPROJ_SEGS = (
    (C_DNQKV, 1536), (C_DNZ, 512), (C_LRUX, 512), (C_LRUY, 512), (C_DAQKV, 1536), (C_GATE, 3072), (C_AB, 128))


def _inproj_kernel(*refs, n_h):
    mod_ref, nw_ref, w_ref = refs[n_h:n_h + 3]
    out_refs = refs[n_h + 3:]
    u = _norm_modulate(_read_stream(refs[:n_h]), nw_ref[...], mod_ref[0:1, :], mod_ref[1:2, :]).astype(BF16)
    for (c0, width), o_ref in zip(_INPROJ_SEGS, out_refs):
        step = min(width, 512)
        for c in range(0, width, step):
            o_ref[:, c:c + step] = _dot(u, w_ref[:, c0 + c:c0 + c + step]).astype(o_ref.dtype)


def _inproj_call(hs, mod, nw, w):
    B = hs[0].shape[0]
    T = sum(h.shape[1] for h in hs)
    nt = T // TM
    dts = (BF16, BF16, BF16, BF16, BF16, BF16, F32)
    return pl.pallas_call(
        functools.partial(_inproj_kernel, n_h=len(hs)),
        grid=(B, nt),
        in_specs=_stream_specs(hs, 0) + [
            pl.BlockSpec((None, None, 6, D_MODEL), _mod_index),
            pl.BlockSpec((1, D_MODEL), lambda b, i: (0, 0)),
            pl.BlockSpec((D_MODEL, IN_COLS_PAD), lambda b, i: (0, 0)),
        ],
        out_specs=[pl.BlockSpec((None, TM, wd), lambda b, i: (b, i, 0)) for _, wd in _INPROJ_SEGS],
        out_shape=[jax.ShapeDtypeStruct((B, T, wd), dt) for (_, wd), dt in zip(_INPROJ_SEGS, dts)],
        compiler_params=_cparams(("arbitrary", "arbitrary")),
        name="in_proj",
    )(*hs, mod, nw, w)


def _halo_valid(i, nt):
    vp = jnp.where(i >= 2, 1.0, 0.0).astype(F32)
    vn = jnp.where(jnp.logical_and(i >= 1, i <= nt - 2), 1.0, 0.0).astype(F32)
    return vp, vn


def _dn_constants():
    r = np.arange(TM)[:, None]
    c = np.arange(TM)[None, :]
    same = (r // DN_CHUNK) == (c // DN_CHUNK)
    shifts = np.stack([c == r - 1, c == r + 1, c == r + 2]).astype(np.float32)
    sums = np.concatenate([same & (c <= r), same], axis=0).astype(np.float32)
    masks = np.stack([same & (c <= r), same & (c >= r), same & (c < r), same & (c > r),
                      (r // 16) == (c // 16), r == c]).astype(np.float32)
    return jnp.asarray(shifts, BF16), jnp.asarray(sums, BF16), jnp.asarray(masks, F32)


def _dn_chunk_kernel(x_ref, xp_ref, xn_ref, ab_ref, cw_ref, al_ref, dtb_ref, sh_ref, sm_ref, mk_ref,
                     wf_ref, uf_ref, qf_ref, kf_ref, qkf_ref, wb_ref, ub_ref, qb_ref, kb_ref, qkb_ref, ge_ref,
                     s_ref, gc_ref, bt_ref, gct_ref, gt_ref, *, nt):
    i = pl.program_id(1)
    vp, vn = _halo_valid(i, nt)
    row8 = lax.broadcasted_iota(jnp.int32, (8, HEAD_W), 0)
    for cb2 in range(3 * N_HEAD // 2):
        c2 = slice(cb2 * 2 * HEAD_W, (cb2 + 1) * 2 * HEAD_W)
        xb = x_ref[:, c2]
        xm1, xp1, xp2 = _dot(sh_ref[0], xb), _dot(sh_ref[1], xb), _dot(sh_ref[2], xb)
        xf = xb.astype(F32)
        for half in range(2):
            cb = cb2 * 2 + half
            cols = slice(cb * HEAD_W, (cb + 1) * HEAD_W)
            hc = slice(half * HEAD_W, (half + 1) * HEAD_W)
            w = cw_ref[:, cols]
            y = w[0:1, :] * xm1[:, hc] + w[1:2, :] * xf[:, hc] + w[2:3, :] * xp1[:, hc] + w[3:4, :] * xp2[:, hc]
            prev = xp_ref[:, cols].astype(F32)[15:16, :] * vp
            nxt = xn_ref[:, cols].astype(F32)
            n0, n1 = nxt[0:1, :] * vn, nxt[1:2, :] * vn
            top = jnp.where(row8 == 0, w[0:1, :] * prev, 0.0)
            bot = jnp.where(row8 == 6, w[3:4, :] * n0, jnp.where(row8 == 7, w[2:3, :] * n0 + w[3:4, :] * n1, 0.0))
            y = jnp.concatenate([y[0:8, :] + top, y[8:TM - 8, :], y[TM - 8:TM, :] + bot], axis=0)
            s = y * _sigmoid(y)
            if cb < 2 * N_HEAD:
                ss = jnp.sum(s * s, axis=-1, keepdims=True)
                s = s * (lax.rsqrt(ss + 1e-6) * (HEAD_W ** -0.5 if cb < N_HEAD else 1.0))
            s_ref[cb] = s

    ab = ab_ref[...]
    lane = lax.broadcasted_iota(jnp.int32, (TM, 128), 1)
    g = -jnp.exp(al_ref[...]) * _softplus(ab + dtb_ref[...])
    g1 = g.astype(BF16)
    r1 = g - g1.astype(F32)
    g2 = r1.astype(BF16)
    g3 = (r1 - g2.astype(F32)).astype(BF16)
    sums = _dot(sm_ref[...], jnp.concatenate([g1, g2, g3], axis=1))
    sums = sums[:, 0:128] + sums[:, 128:256] + sums[:, 256:384]
    g_f = sums[0:TM, :]
    gtot = sums[TM:2 * TM, :]
    gc = jnp.where(lane < N_HEAD, g_f, gtot - g_f + g)
    gc_ref[...] = gc
    bt_ref[...] = _sigmoid(ab)
    gt_ref[...] = gtot
    gct_ref[...] = gc.T
    ge_ref[...] = jnp.exp(gtot)
    out_refs = ((wf_ref, uf_ref, qf_ref, kf_ref, qkf_ref), (wb_ref, ub_ref, qb_ref, kb_ref, qkb_ref))
    sub8 = lax.broadcasted_iota(jnp.int32, (8, TM), 0)

    probs = [(h, d) for h in range(N_HEAD) for d in range(2)]
    kk, qk0 = {}, {}
    for h in range(N_HEAD):
        kb16 = s_ref[N_HEAD + h].astype(BF16)
        kk[h] = _dot_nt(kb16, kb16)
        qk0[h] = _dot_nt(s_ref[h].astype(BF16), kb16)
    tmat, dblk, lo_b, rhs_b = {}, {}, {}, {}
    for h, d in probs:
        q = s_ref[h]
        k = s_ref[N_HEAD + h]
        v = s_ref[2 * N_HEAD + h]
        idx = d * N_HEAD + h
        sel = lane == idx
        gc_col = jnp.sum(jnp.where(sel, gc_ref[...], 0.0), axis=1, keepdims=True)
        gt_col = jnp.sum(jnp.where(sel, gt_ref[...], 0.0), axis=1, keepdims=True)
        bt_col = jnp.sum(jnp.where(lane == idx + 2 * N_HEAD, bt_ref[...], 0.0), axis=1, keepdims=True)
        gc_row = jnp.sum(jnp.where(sub8 == idx, gct_ref[0:8, :], 0.0), axis=0, keepdims=True)
        e = jnp.exp(jnp.minimum(gc_col - gc_row, 0.0))
        a = kk[h] * (e * mk_ref[2 + d] * bt_col)
        eg = jnp.exp(gc_col)
        _, _, q_r, k_r, qk_r = out_refs[d]
        q_r[h] = (q * eg).astype(BF16)
        k_r[h] = (k * jnp.exp(gt_col - gc_col)).astype(BF16)
        qk_r[h] = (qk0[h] * (e * mk_ref[d])).astype(BF16)
        dm = a * mk_ref[4]
        tmat[h, d] = mk_ref[5] - dm
        lo_b[h, d] = (a - dm).astype(BF16)
        rhs_b[h, d] = jnp.concatenate([k * (bt_col * eg), v * bt_col], axis=1).astype(BF16)
        dblk[h, d] = dm.astype(BF16)
    pw = {pk: _dot(dblk[pk], dblk[pk]) for pk in probs}
    for m in range(3):
        for pk in probs:
            pb = pw[pk].astype(BF16)
            tmat[pk] = tmat[pk] + _dot(tmat[pk].astype(BF16), pb)
            if m < 2:
                pw[pk] = _dot(pb, pb)
    x0, mt, m2 = {}, {}, {}
    for pk in probs:
        tb = tmat[pk].astype(BF16)
        x0[pk] = _dot(tb, rhs_b[pk])
        mt[pk] = _dot(tb, lo_b[pk]).astype(BF16)
    for pk in probs:
        m2[pk] = _dot(mt[pk], mt[pk]).astype(BF16)
    ys = {pk: x0[pk] + _dot(m2[pk], x0[pk].astype(BF16)) for pk in probs}
    for h, d in probs:
        x = ys[h, d] - _dot(mt[h, d], ys[h, d].astype(BF16))
        w_r, u_r = out_refs[d][0], out_refs[d][1]
        w_r[h] = x[:, :HEAD_W].astype(BF16)
        u_r[h] = x[:, HEAD_W:]


def _dn_chunk_call(qkv, ab, conv_w, a_log, dt_bias):
    B, T, W = qkv.shape
    nt = T // TM
    hb = TM // 16
    last16 = T // 16 - 1
    al = jnp.zeros((1, 128), F32).at[0, :2 * N_HEAD].set(a_log.reshape(-1))
    dtb = jnp.zeros((1, 128), F32).at[0, :2 * N_HEAD].set(dt_bias.reshape(-1))
    shifts, sums, masks = _dn_constants()
    hspec = lambda w: pl.BlockSpec((None, N_HEAD, TM, w), lambda b, i: (b, 0, i, 0))
    hshape = lambda w, dt: jax.ShapeDtypeStruct((B, N_HEAD, T, w), dt)
    const = lambda a: pl.BlockSpec(a.shape, lambda b, i: (0,) * a.ndim)
    per_dir_specs = [hspec(HEAD_W), hspec(HEAD_W), hspec(HEAD_W), hspec(HEAD_W), hspec(TM)]
    per_dir_shapes = [hshape(HEAD_W, BF16), hshape(HEAD_W, F32), hshape(HEAD_W, BF16), hshape(HEAD_W, BF16),
                      hshape(TM, BF16)]
    return pl.pallas_call(
        functools.partial(_dn_chunk_kernel, nt=nt),
        grid=(B, nt),
        in_specs=[
            pl.BlockSpec((None, TM, W), lambda b, i: (b, i, 0)),
            pl.BlockSpec((None, 16, W), lambda b, i: (b, jnp.maximum(i * hb - 1, 0), 0)),
            pl.BlockSpec((None, 16, W), lambda b, i: (b, jnp.minimum((i + 1) * hb, last16), 0)),
            pl.BlockSpec((None, TM, 128), lambda b, i: (b, i, 0)),
            pl.BlockSpec((4, W), lambda b, i: (0, 0)),
            pl.BlockSpec((1, 128), lambda b, i: (0, 0)),
            pl.BlockSpec((1, 128), lambda b, i: (0, 0)),
            const(shifts), const(sums), const(masks),
        ],
        out_specs=per_dir_specs + per_dir_specs + [pl.BlockSpec((None, TM, 128), lambda b, i: (b, i, 0))],
        out_shape=per_dir_shapes + per_dir_shapes + [jax.ShapeDtypeStruct((B, T, 128), F32)],
        scratch_shapes=[
            pltpu.VMEM((3 * N_HEAD, TM, HEAD_W), F32),
            pltpu.VMEM((TM, 128), F32), pltpu.VMEM((TM, 128), F32),
            pltpu.VMEM((128, TM), F32), pltpu.VMEM((TM, 128), F32),
        ],
        compiler_params=_cparams(("arbitrary", "arbitrary")),
        name="dn_chunk",
    )(qkv, qkv, qkv, ab, conv_w, al, dtb, shifts, sums, masks)


def _dn_seq_kernel(wf_ref, uf_ref, qf_ref, kf_ref, qkf_ref, gef_ref, wb_ref, ub_ref, qb_ref, kb_ref, qkb_ref,
                   geb_ref, of_ref, ob_ref, st_ref):
    C = DN_CHUNK
    nc = TM // C

    @pl.when(pl.program_id(1) == 0)
    def _():
        st_ref[...] = jnp.zeros_like(st_ref)

    ins = ((wf_ref, uf_ref, qf_ref, kf_ref, qkf_ref, gef_ref, of_ref),
           (wb_ref, ub_ref, qb_ref, kb_ref, qkb_ref, geb_ref, ob_ref))
    chains = [(d, bb, h) for d in range(2) for bb in range(DN_SEQ_BB) for h in range(N_HEAD)]
    for ci in range(nc):
        st, sb, ws, qs, vb = {}, {}, {}, {}, {}
        for ch in chains:
            d, bb, h = ch
            c = ci if d == 0 else nc - 1 - ci
            rows = slice(c * C, (c + 1) * C)
            st[ch] = st_ref[(d * DN_SEQ_BB + bb) * N_HEAD + h]
            sb[ch] = st[ch].astype(BF16)
            ws[ch] = _dot(ins[d][0][bb, h, rows, :], sb[ch])
            qs[ch] = _dot(ins[d][2][bb, h, rows, :], sb[ch])
        for ch in chains:
            d, bb, h = ch
            c = ci if d == 0 else nc - 1 - ci
            rows = slice(c * C, (c + 1) * C)
            vb[ch] = (ins[d][1][bb, h, rows, :] - ws[ch]).astype(BF16)
        for ch in chains:
            d, bb, h = ch
            c = ci if d == 0 else nc - 1 - ci
            rows = slice(c * C, (c + 1) * C)
            _, _, _, k_r, qk_r, ge_r, o_r = ins[d]
            si = d * N_HEAD + h
            ge = ge_r[bb, c * C:c * C + 8, :][0:1, si:si + 1]
            o_r[bb, rows, h * HEAD_W:(h + 1) * HEAD_W] = qs[ch] + _dot(qk_r[bb, h, rows, c * C:(c + 1) * C], vb[ch])
            st_ref[(d * DN_SEQ_BB + bb) * N_HEAD + h] = ge * st[ch] + _dot_tn(k_r[bb, h, rows, :], vb[ch])


def _dn_seq_call(chunk_outs):
    wf, uf, qf, kf, qkf, wb, ub, qb, kb, qkb, ge = chunk_outs
    B, _, T, _ = wf.shape
    nt = T // TM
    bb = DN_SEQ_BB
    tf = lambda s: s
    tb = lambda s: jnp.where(s == 0, 0, nt - s)
    hspec = lambda w, t: pl.BlockSpec((bb, N_HEAD, TM, w), lambda b, s: (b, 0, t(s), 0))
    gspec = lambda t: pl.BlockSpec((bb, TM, 128), lambda b, s: (b, t(s), 0))
    dir_specs = lambda t: [hspec(HEAD_W, t), hspec(HEAD_W, t), hspec(HEAD_W, t), hspec(HEAD_W, t), hspec(TM, t),
                           gspec(t)]
    ospec = lambda t: pl.BlockSpec((bb, TM, MIX_W), lambda b, s: (b, t(s), 0))
    return pl.pallas_call(
        _dn_seq_kernel,
        grid=(B // bb, nt),
        in_specs=dir_specs(tf) + dir_specs(tb),
        out_specs=[ospec(tf), ospec(tb)],
        out_shape=[jax.ShapeDtypeStruct((B, T, MIX_W), F32)] * 2,
        scratch_shapes=[pltpu.VMEM((2 * bb * N_HEAD, HEAD_W, HEAD_W), F32)],
        compiler_params=_cparams(("arbitrary", "arbitrary")),
        name="dn_seq",
    )(wf, uf, qf, kf, qkf, ge, wb, ub, qb, kb, qkb, ge)


def _gelu_tanh(x):
    return 0.5 * x * (1.0 + jnp.tanh(math.sqrt(2.0 / math.pi) * (x + 0.044715 * (x * x * x))))


def _lru_kernel(x_ref, y_ref, cw_ref, cb_ref, wg_ref, bg_ref, lam_ref, sh_ref, o_ref, hf_ref, xc_ref, *, nt):
    CW = x_ref.shape[1]
    T = x_ref.shape[0]
    G = TM // 8
    row8 = lax.broadcasted_iota(jnp.int32, (8, CW), 0)
    sub = lax.broadcasted_iota(jnp.int32, (G, 8, CW), 1)

    def conv(t):
        base = pl.multiple_of(t * TM, TM)
        xb = x_ref[pl.ds(base, TM), :]
        xp = x_ref[pl.ds(pl.multiple_of(jnp.maximum(base - 16, 0), 16), 16), :].astype(F32)
        xn = x_ref[pl.ds(pl.multiple_of(jnp.minimum(base + TM, T - 16), 16), 16), :].astype(F32)
        vp, vn = _halo_valid(t, nt)
        w = cw_ref[...]
        y = (w[0:1, :] * _dot(sh_ref[0], xb) + w[1:2, :] * xb.astype(F32)
             + w[2:3, :] * _dot(sh_ref[1], xb) + w[3:4, :] * _dot(sh_ref[2], xb))
        n0, n1 = xn[0:1, :] * vn, xn[1:2, :] * vn
        top = jnp.where(row8 == 0, w[0:1, :] * (xp[15:16, :] * vp), 0.0)
        bot = jnp.where(row8 == 6, w[3:4, :] * n0, jnp.where(row8 == 7, w[2:3, :] * n0 + w[3:4, :] * n1, 0.0))
        return jnp.concatenate([y[0:8, :] + top, y[8:TM - 8, :], y[TM - 8:TM, :] + bot], axis=0) + cb_ref[...]

    def coeffs(xc, d):
        z = _dot(xc.astype(BF16), wg_ref[:, d * 2 * CW:(d + 1) * 2 * CW]) + bg_ref[:, d * 2 * CW:(d + 1) * 2 * CW]
        rg = _sigmoid(z[:, :CW])
        ig = _sigmoid(z[:, CW:])
        log_a = (-LRU_C * _softplus(-lam_ref[d:d + 1, :])) * rg
        a = jnp.exp(log_a)
        om = 1.0 - a * a
        b = jnp.where(om > 0.0, om * lax.rsqrt(om), 0.0) * (ig * xc)
        return a, b

    def scan_tile(a, b, h_in, d):
        a = a.reshape(G, 8, CW)
        b = b.reshape(G, 8, CW)
        for s in (1, 2, 4):
            if d == 0:
                a_s, b_s, valid = pltpu.roll(a, s, 1), pltpu.roll(b, s, 1), sub >= s
            else:
                a_s, b_s, valid = pltpu.roll(a, 8 - s, 1), pltpu.roll(b, 8 - s, 1), sub < 8 - s
            b = jnp.where(valid, a * b_s + b, b)
            a = jnp.where(valid, a * a_s, a)
        hs = [None] * G
        h = h_in
        for g in (range(G) if d == 0 else range(G - 1, -1, -1)):
            hg = b[g] + a[g] * h
            hs[g] = hg
            h = hg[7:8, :] if d == 0 else hg[0:1, :]
        return jnp.concatenate(hs, axis=0), h

    def fwd(t, h):
        rows = pl.ds(pl.multiple_of(t * TM, TM), TM)
        xc = conv(t)
        xc_ref[rows, :] = xc
        a, b = coeffs(xc, 0)
        hf, h = scan_tile(a, b, h, 0)
        hf_ref[rows, :] = hf
        return h

    lax.fori_loop(0, nt, fwd, jnp.zeros((1, CW), F32))

    def bwd(s, h):
        t = jnp.where(s == 0, 0, nt - s)
        rows = pl.ds(pl.multiple_of(t * TM, TM), TM)
        a, b = coeffs(xc_ref[rows, :], 1)
        hb, h = scan_tile(a, b, h, 1)
        y = y_ref[rows, :].astype(F32)
        o_ref[rows, :] = ((hf_ref[rows, :] + hb) * _gelu_tanh(y)).astype(o_ref.dtype)
        return h

    lax.fori_loop(0, nt, bwd, jnp.zeros((1, CW), F32))


def _lru_call(x, y, conv_w, conv_b, wg, bg, lam):
    B, T, W = x.shape
    cw = W // 2
    nt = T // TM
    shifts = _dn_constants()[0]
    return pl.pallas_call(
        functools.partial(_lru_kernel, nt=nt),
        grid=(B, 2),
        in_specs=[
            pl.BlockSpec((None, T, cw), lambda b, p: (b, 0, p)),
            pl.BlockSpec((None, T, cw), lambda b, p: (b, 0, p)),
            pl.BlockSpec((4, cw), lambda b, p: (0, p)),
            pl.BlockSpec((1, cw), lambda b, p: (0, p)),
            pl.BlockSpec((None, cw, 4 * cw), lambda b, p: (p, 0, 0)),
            pl.BlockSpec((None, 1, 4 * cw), lambda b, p: (p, 0, 0)),
            pl.BlockSpec((2, cw), lambda b, p: (0, p)),
            pl.BlockSpec(shifts.shape, lambda b, p: (0, 0, 0)),
        ],
        out_specs=pl.BlockSpec((None, T, cw), lambda b, p: (b, 0, p)),
        out_shape=jax.ShapeDtypeStruct((B, T, W), BF16),
        scratch_shapes=[pltpu.VMEM((T, cw), F32), pltpu.VMEM((T, cw), F32)],
        compiler_params=_cparams(("arbitrary", "arbitrary")),
        name="rglru",
    )(x, y, conv_w, conv_b, wg, bg, lam, shifts)


def _rope(x, cs, sn, first_half):
    swapped = jnp.where(first_half, pltpu.roll(x, 112, 1), pltpu.roll(x, 16, 1))
    return x * cs + swapped * sn


def _attn_kernel(*refs, lam_init, n_q, nt):
    q_refs = refs[:n_q]
    k_ref, v_ref = refs[n_q:n_q + 2]
    tab_refs = refs[n_q + 2:3 * n_q + 2]
    ck_ref, sk_ref, lv_ref, nw_ref, o_ref, kr_ref, vt_ref, qt_ref, st0_ref, st1_ref, acc_ref = refs[3 * n_q + 2:]
    lane = lax.broadcasted_iota(jnp.int32, (TM, HEAD_W), 1)
    first_half = (lane % 32) < 16
    heads = [(hh, slice(hh * HEAD_W, (hh + 1) * HEAD_W)) for hh in range(ATTN_HP)]
    chains = [(qi, hh) for qi in range(n_q) for hh in range(ATTN_HP)]

    @pl.when(pl.program_id(2) == 0)
    def _():
        extra = (lax.broadcasted_iota(jnp.int32, (VT_ROWS - HEAD_W, TM), 0) == 0).astype(BF16)

        def prep_kv(t, carry):
            rows = pl.ds(pl.multiple_of(t * TM, TM), TM)
            for hh, cols in heads:
                kr_ref[hh, rows, :] = _rope(k_ref[rows, cols].astype(F32), ck_ref[rows, :], sk_ref[rows, :],
                                            first_half).astype(BF16)
                vt_ref[hh, 0:HEAD_W, rows] = v_ref[rows, cols].astype(F32).T.astype(BF16)
                vt_ref[hh, HEAD_W:VT_ROWS, rows] = extra
            return carry
        lax.fori_loop(0, nt, prep_kv, 0)

    for qi, hh in chains:
        q = _rope(q_refs[qi][:, heads[hh][1]].astype(F32), tab_refs[2 * qi][...], tab_refs[2 * qi + 1][...],
                  first_half) * (64 ** -0.5 * math.log2(math.e))
        qt_ref[qi * ATTN_HP + hh] = jnp.concatenate(
            [jnp.where(lane < 64, q, 0.0).T, jnp.where(lane >= 64, q, 0.0).T], axis=1).astype(BF16)
        acc_ref[qi * ATTN_HP + hh] = jnp.zeros((VT_ROWS, 2 * TM), F32)

    blocks = [slice(0, TM)] + [slice(s0 * TM, min(s0 + ATTN_KB, nt) * TM) for s0 in range(1, nt, ATTN_KB)]

    def scores(keys, st_ref):
        nk = keys.stop - keys.start
        mx = {}
        for qi, hh in chains:
            ci = qi * ATTN_HP + hh
            st = _dot(kr_ref[hh, keys, :], qt_ref[ci])
            st_ref[ci, 0:nk, :] = st
            mx[ci] = jnp.max(st, axis=0, keepdims=True)
        return mx

    def softmax_pv(keys, st_ref, ms, mx):
        nk = keys.stop - keys.start
        out = {}
        for qi, hh in chains:
            ci = qi * ATTN_HP + hh
            m_new = jnp.maximum(ms[ci], mx[ci])
            alpha = jnp.exp2(ms[ci] - m_new)
            for half in range(2):
                hc = slice(half * TM, (half + 1) * TM)
                p = jnp.exp2(st_ref[ci, 0:nk, hc] - m_new[:, hc]).astype(BF16)
                acc_ref[ci, :, hc] = alpha[:, hc] * acc_ref[ci, :, hc] + _dot(vt_ref[hh, :, keys], p)
            out[ci] = m_new
        return out

    bufs = (st0_ref, st1_ref)
    mx = scores(blocks[0], st0_ref)
    ms = {qi * ATTN_HP + hh: jnp.full((1, 2 * TM), -1e30, F32) for qi, hh in chains}
    for j, keys in enumerate(blocks):
        mx_next = scores(blocks[j + 1], bufs[(j + 1) % 2]) if j + 1 < len(blocks) else None
        ms = softmax_pv(keys, bufs[j % 2], ms, mx)
        mx = mx_next
    lv = lv_ref[...]
    lam = (jnp.exp(jnp.sum(lv[0:1, :] * lv[1:2, :], axis=-1, keepdims=True))
           - jnp.exp(jnp.sum(lv[2:3, :] * lv[3:4, :], axis=-1, keepdims=True)) + lam_init)
    for qi, hh in chains:
        ci = qi * ATTN_HP + hh
        on = acc_ref[ci, 0:HEAD_W, :] * (1.0 / acc_ref[ci, HEAD_W:HEAD_W + 1, :])
        o = (on[:, :TM] - lam * on[:, TM:]).T
        ms_o = jnp.mean(o * o, axis=-1, keepdims=True)
        o_ref[qi * TM:(qi + 1) * TM, heads[hh][1]] = (
            o * lax.rsqrt(ms_o + 1e-5) * nw_ref[...] * (1.0 - lam_init)).astype(o_ref.dtype)


def _attn_call(qkv, cos_t, sin_t, lam_vecs, sub_norm, lam_init, ctx):
    B, T, _ = qkv.shape
    pw = ATTN_HP * HEAD_W
    npair = N_HEAD // ATTN_HP
    if ctx:
        n_q, nk, nsteps, q_tiles = 1, 1, 1, [lambda s: 0]
    else:
        assert (T // TM - 1) % 2 == 0, "latent query tiles are processed in pairs"
        n_q, nk, nsteps = 2, T // TM, (T // TM - 1) // 2
        q_tiles = [lambda s: 2 * s + 1, lambda s: 2 * s + 2]
    nch = n_q * ATTN_HP
    qspec = lambda t: pl.BlockSpec((None, TM, pw), lambda b, h, s: (b, t(s), h))
    tspec = lambda t: pl.BlockSpec((TM, HEAD_W), lambda b, h, s: (t(s), 0))
    tabs = [tspec(t) for t in q_tiles for _ in range(2)]
    kern = functools.partial(_attn_kernel, lam_init=lam_init, n_q=n_q, nt=nk)
    return pl.pallas_call(
        kern,
        grid=(B, npair, nsteps),
        in_specs=[qspec(t) for t in q_tiles] + [
            pl.BlockSpec((None, nk * TM, pw), lambda b, h, s: (b, 0, npair + h)),
            pl.BlockSpec((None, nk * TM, pw), lambda b, h, s: (b, 0, 2 * npair + h)),
        ] + tabs + [
            pl.BlockSpec((nk * TM, HEAD_W), lambda b, h, s: (0, 0)),
            pl.BlockSpec((nk * TM, HEAD_W), lambda b, h, s: (0, 0)),
            pl.BlockSpec((4, 64), lambda b, h, s: (0, 0)),
            pl.BlockSpec((1, HEAD_W), lambda b, h, s: (0, 0)),
        ],
        out_specs=pl.BlockSpec((None, n_q * TM, pw), lambda b, h, s: (b, s, h)),
        out_shape=jax.ShapeDtypeStruct((B, nsteps * n_q * TM, MIX_W), BF16),
        scratch_shapes=[pltpu.VMEM((ATTN_HP, nk * TM, HEAD_W), BF16), pltpu.VMEM((ATTN_HP, VT_ROWS, nk * TM), BF16),
                        pltpu.VMEM((nch, HEAD_W, 2 * TM), BF16),
                        pltpu.VMEM((nch, ATTN_KB * TM, 2 * TM), F32), pltpu.VMEM((nch, ATTN_KB * TM, 2 * TM), F32),
                        pltpu.VMEM((nch, VT_ROWS, 2 * TM), F32)],
        compiler_params=_cparams(("arbitrary", "arbitrary", "arbitrary")),
        name="diff_attn_ctx" if ctx else "diff_attn",
    )(*([qkv] * (n_q + 2)), *([cos_t, sin_t] * n_q), cos_t, sin_t, lam_vecs, sub_norm)


def _mix_ffn_kernel(*refs, n_h, t_off, final):
    mod_ref, dnf_ref, dnb_ref, z_ref, yb_ref = refs[n_h:n_h + 5]
    rest = refs[n_h + 5:]
    (ycl_ref, g_ref, dnw_ref, wb_ref, wo_ref, nw_ref, wg_ref, wu_ref, wd_ref, fw_ref, o_ref) = rest[-11:]
    yc = ycl_ref[...]
    if len(rest) == 12:
        yc = jnp.where(pl.program_id(1) + t_off == 0, rest[0][...], yc)
    dnw = dnw_ref[...]
    ya_parts = []
    for hh in range(N_HEAD):
        cols = slice(hh * HEAD_W, (hh + 1) * HEAD_W)
        o = dnf_ref[:, cols] + dnb_ref[:, cols]
        ms = jnp.mean(o * o, axis=-1, keepdims=True)
        z = z_ref[:, cols].astype(F32)
        ya_parts.append(((o * lax.rsqrt(ms + 1e-6) * dnw) * (z * _sigmoid(z))).astype(BF16))
    ys = (jnp.concatenate(ya_parts, axis=1), yb_ref[...], yc)
    mix = None
    for n in range(3):
        up = _dot(ys[n], wb_ref[n])
        term = _sigmoid(g_ref[:, n * D_MODEL:(n + 1) * D_MODEL].astype(F32)) * up
        mix = term if mix is None else mix + term
    h = _read_stream(refs[:n_h]) + mod_ref[2:3, :] * _dot(mix.astype(BF16), wo_ref[...])

    x = _norm_modulate(h, nw_ref[...], mod_ref[3:4, :], mod_ref[4:5, :]).astype(BF16)
    acc = None
    for c in range(0, D_FF, FF_CHUNK):
        ce = min(c + FF_CHUNK, D_FF)
        g = _dot(x, wg_ref[:, c:ce])
        u = _dot(x, wu_ref[:, c:ce])
        part = _dot(((g * _sigmoid(g)) * u).astype(BF16), wd_ref[c:ce, :])
        acc = part if acc is None else acc + part
    out = h + mod_ref[5:6, :] * acc
    if final:
        ms = jnp.mean(out * out, axis=-1, keepdims=True)
        out = out * lax.rsqrt(ms + 1e-6) * fw_ref[...]
    o_ref[...] = out


def _mix_ffn_call(hs, mod, dnf, dnb, z, yb, yc_ctx, yc_lat, gates, dn_norm, wb, wo, nw, wg, wu, wd, fw, t_off, final):
    B = hs[0].shape[0]
    nt = sum(h.shape[1] for h in hs) // TM - t_off
    row = lambda w: pl.BlockSpec((None, TM, w), lambda b, i: (b, i + t_off, 0))
    const = lambda a: pl.BlockSpec(a.shape, lambda b, i: (0,) * a.ndim, pipeline_mode=pl.Buffered(1))
    yc_specs = [pl.BlockSpec((None, TM, MIX_W), lambda b, i: (b, jnp.maximum(i + t_off - 1, 0), 0))]
    yc_args = [yc_lat]
    if yc_ctx is not None:
        yc_specs.insert(0, pl.BlockSpec((None, TM, MIX_W), lambda b, i: (b, 0, 0)))
        yc_args.insert(0, yc_ctx)
    return pl.pallas_call(
        functools.partial(_mix_ffn_kernel, n_h=len(hs), t_off=t_off, final=final),
        grid=(B, nt),
        in_specs=_stream_specs(hs, t_off) + [
            pl.BlockSpec((None, None, 6, D_MODEL), lambda b, i: _mod_index(b, i + t_off)),
            row(MIX_W), row(MIX_W), row(MIX_W), row(MIX_W), *yc_specs, row(3 * D_MODEL),
            const(dn_norm), const(wb), const(wo), const(nw), const(wg), const(wu), const(wd), const(fw),
        ],
        out_specs=pl.BlockSpec((None, TM, D_MODEL), lambda b, i: (b, i, 0)),
        out_shape=jax.ShapeDtypeStruct((B, nt * TM, D_MODEL), F32),
        compiler_params=_cparams(("arbitrary", "arbitrary")),
        name="mix_ffn",
    )(*hs, mod, dnf, dnb, z, yb, *yc_args, gates, dn_norm, wb, wo, nw, wg, wu, wd, fw)


def _reorder_w_in(w):
    parts = [w[:, 0:1536], w[:, 1536:2048], w[:, 2064:2576], w[:, 2576:3088], w[:, 3088:4624], w[:, 4624:7696],
             w[:, 2048:2064], jnp.zeros((w.shape[0], IN_COLS_PAD - 7696), w.dtype)]
    return jnp.concatenate(parts, axis=1).astype(BF16)


def _block_diag(w):
    g, n, _ = w.shape
    return jnp.einsum('gij,gh->gihj', w, jnp.eye(g, dtype=w.dtype)).reshape(g * n, g * n)


def _lru_gate_weights(wa, ba, wi, bi):
    half = LRU_G // 2
    ws, bs = [], []
    for p in range(2):
        grp = slice(p * half, (p + 1) * half)
        ch = slice(p * MIX_W // 2, (p + 1) * MIX_W // 2)
        cols, bcols = [], []
        for d in range(2):
            cols += [_block_diag(wa[d, grp]), _block_diag(wi[d, grp])]
            bcols += [ba[d, ch], bi[d, ch]]
        ws.append(jnp.concatenate(cols, axis=1))
        bs.append(jnp.concatenate(bcols)[None, :])
    return jnp.stack(ws).astype(BF16), jnp.stack(bs)


def _rope_tables(seq):
    rows = seq // GRID_W
    rowp = jnp.repeat(jnp.arange(rows, dtype=F32), GRID_W)
    colp = jnp.tile(jnp.arange(GRID_W, dtype=F32), rows)
    inv = ROPE_BASE ** (-jnp.arange(ROPE_NF, dtype=F32) / ROPE_NF)
    ar, ac = rowp[:, None] * inv, colp[:, None] * inv
    cs = jnp.concatenate([jnp.cos(ar), jnp.cos(ar), jnp.cos(ac), jnp.cos(ac)], axis=1)
    sn = jnp.concatenate([-jnp.sin(ar), jnp.sin(ar), -jnp.sin(ac), jnp.sin(ac)], axis=1)
    cs = jnp.concatenate([jnp.ones((CTX_LEN, 64), F32), cs], axis=0)
    sn = jnp.concatenate([jnp.zeros((CTX_LEN, 64), F32), sn], axis=0)
    return jnp.tile(cs, (1, 2)), jnp.tile(sn, (1, 2))


def kernel(x, c, ctx, c_ctx, w_mod, b_mod, norm_mix, norm_ffn, w_in, dn_conv, dn_a_log, dn_dt_bias, dn_norm, lru_conv_w, lru_conv_b, lru_wa, lru_ba, lru_wi, lru_bi, lru_lambda, da_lambda, da_norm, w_branch, w_out, w_ffn_gate, w_ffn_up, w_ffn_down, norm_final):
    B, S, D = x.shape
    depth = w_mod.shape[0]
    rows = jnp.concatenate([c, c_ctx[None, :], jnp.zeros((16 - B - 1, D), F32)], axis=0)
    mod = _mod_call(rows, w_mod, b_mod).reshape(depth, 16, 6, D)
    cos_t, sin_t = _rope_tables(S)
    hs = (ctx, x)
    for l in range(depth):
        last = l == depth - 1
        t_off = 1 if last else 0
        lam_init = 0.8 - 0.6 * math.exp(-0.3 * l)
        mod_l = jnp.stack([jnp.broadcast_to(mod[l, B][None], (B, 6, D)), mod[l, :B]], axis=1)
        dn_qkv, dn_z, lru_x, lru_y, da_qkv, gates, ab = _inproj_call(
            hs, mod_l, norm_mix[l][None, :], _reorder_w_in(w_in[l]))
        dn_f, dn_b = _dn_seq_call(_dn_chunk_call(dn_qkv, ab, dn_conv[l], dn_a_log[l], dn_dt_bias[l]))
        wg, bg = _lru_gate_weights(lru_wa[l], lru_ba[l], lru_wi[l], lru_bi[l])
        yb = _lru_call(lru_x, lru_y, lru_conv_w[l], lru_conv_b[l][None, :], wg, bg, lru_lambda[l])
        attn = functools.partial(_attn_call, da_qkv, cos_t, sin_t, da_lambda[l], da_norm[l][None, :], lam_init)
        yc_ctx, yc_lat = (None if last else attn(True)), attn(False)
        hs = (_mix_ffn_call(hs, mod_l, dn_f, dn_b, dn_z, yb, yc_ctx, yc_lat, gates, dn_norm[l][None, :],
                            w_branch[l].astype(BF16), w_out[l].astype(BF16), norm_ffn[l][None, :],
                            w_ffn_gate[l].astype(BF16), w_ffn_up[l].astype(BF16), w_ffn_down[l].astype(BF16),
                            norm_final[None, :], t_off, last),)
    return hs[0]
```

```python
import functools
import math

import jax
import jax.numpy as jnp
import numpy as np
from jax import lax
from jax.experimental import pallas as pl
from jax.experimental.pallas import tpu as pltpu

F32 = jnp.float32
BF16 = jnp.bfloat16

D_MODEL = 1024
CTX_LEN = 256
GRID_W = 64
MIX_W = 512
N_HEAD = 4
HEAD_W = 128
DN_CHUNK = 64
LRU_C = 8.0
LRU_G = 8
ROPE_BASE = 10000.0
ROPE_NF = 16
D_FF = 2816
TM = 256
FF_CHUNK = 1024
VT_ROWS = HEAD_W + 16
ATTN_HP = 2
ATTN_KB = 1
DN_SEQ_BB = 4
VMEM_LIMIT = 56 * 1024 * 1024

C_DNQKV, C_DNZ, C_LRUX, C_LRUY, C_DAQKV, C_GATE, C_AB = 0, 1536, 2048, 2560, 3072, 4608, 7680
IN_COLS_PAD = 7808


def _cparams(sem):
    return pltpu.CompilerParams(dimension_semantics=sem, vmem_limit_bytes=VMEM_LIMIT)


def _sigmoid(x):
    return 1.0 / (1.0 + jnp.exp(-x))


def _softplus(x):
    return jnp.maximum(x, 0.0) + jnp.log(1.0 + jnp.exp(-jnp.abs(x)))


def _dot(a, b):
    return jnp.dot(a, b, preferred_element_type=F32)


def _dot_nt(a, b):
    return lax.dot_general(a, b, (((1,), (1,)), ((), ())), preferred_element_type=F32)


def _dot_tn(a, b):
    return lax.dot_general(a, b, (((0,), (0,)), ((), ())), preferred_element_type=F32)


def _mod_kernel(s_ref, w_ref, b_ref, o_ref):
    s = s_ref[...]
    s = s * _sigmoid(s)
    o_ref[...] = jnp.dot(s, w_ref[...], preferred_element_type=F32,
                         precision=lax.Precision.HIGHEST) + b_ref[...]


def _mod_call(rows, w_mod, b_mod):
    depth = w_mod.shape[0]
    n6 = w_mod.shape[2]
    tn = 1024
    return pl.pallas_call(
        _mod_kernel,
        grid=(depth, n6 // tn),
        in_specs=[
            pl.BlockSpec((16, D_MODEL), lambda l, j: (0, 0)),
            pl.BlockSpec((None, D_MODEL, tn), lambda l, j: (l, 0, j)),
            pl.BlockSpec((None, 1, tn), lambda l, j: (l, 0, j)),
        ],
        out_specs=pl.BlockSpec((None, 16, tn), lambda l, j: (l, 0, j)),
        out_shape=jax.ShapeDtypeStruct((depth, 16, n6), F32),
        compiler_params=_cparams(("arbitrary", "arbitrary")),
        name="adaln_mod",
    )(rows, w_mod, b_mod.reshape(depth, 1, n6))


def _mod_index(b, i):
    return (b, jnp.minimum(i, 1), 0, 0)


def _stream_specs(parts, t_off):
    if len(parts) == 1:
        return [pl.BlockSpec((None, TM, D_MODEL), lambda b, i: (b, i + t_off, 0))]
    assert t_off == 0
    return [pl.BlockSpec((None, TM, D_MODEL), lambda b, i: (b, 0, 0)),
            pl.BlockSpec((None, TM, D_MODEL), lambda b, i: (b, jnp.maximum(i - 1, 0), 0))]


def _read_stream(refs):
    if len(refs) == 1:
        return refs[0][...]
    return jnp.where(pl.program_id(1) == 0, refs[0][...], refs[1][...])


def _norm_modulate(x, nw, shift, scale, eps=1e-6):
    ms = jnp.mean(x * x, axis=-1, keepdims=True)
    y = x * lax.rsqrt(ms + eps) * nw
    return y * (1.0 + scale) + shift


_INPROJ_SEGS = (
    (C_DNQKV, 1536), (C_DNZ, 512), (C_LRUX, 512), (C_LRUY, 512), (C_DAQKV, 1536), (C_GATE, 3072), (C_AB, 128))


def _inproj_kernel(*refs, n_h):
    mod_ref, nw_ref, w_ref = refs[n_h:n_h + 3]
    out_refs = refs[n_h + 3:]
    u = _norm_modulate(_read_stream(refs[:n_h]), nw_ref[...], mod_ref[0:1, :], mod_ref[1:2, :]).astype(BF16)
    for (c0, width), o_ref in zip(_INPROJ_SEGS, out_refs):
        step = min(width, 512)
        for c in range(0, width, step):
            o_ref[:, c:c + step] = _dot(u, w_ref[:, c0 + c:c0 + c + step]).astype(o_ref.dtype)


def _inproj_call(hs, mod, nw, w):
    B = hs[0].shape[0]
    T = sum(h.shape[1] for h in hs)
    nt = T // TM
    dts = (BF16, BF16, BF16, BF16, BF16, BF16, F32)
    return pl.pallas_call(
        functools.partial(_inproj_kernel, n_h=len(hs)),
        grid=(B, nt),
        in_specs=_stream_specs(hs, 0) + [
            pl.BlockSpec((None, None, 6, D_MODEL), _mod_index),
            pl.BlockSpec((1, D_MODEL), lambda b, i: (0, 0)),
            pl.BlockSpec((D_MODEL, IN_COLS_PAD), lambda b, i: (0, 0)),
        ],
        out_specs=[pl.BlockSpec((None, TM, wd), lambda b, i: (b, i, 0)) for _, wd in _INPROJ_SEGS],
        out_shape=[jax.ShapeDtypeStruct((B, T, wd), dt) for (_, wd), dt in zip(_INPROJ_SEGS, dts)],
        compiler_params=_cparams(("arbitrary", "arbitrary")),
        name="in_proj",
    )(*hs, mod, nw, w)


def _halo_valid(i, nt):
    vp = jnp.where(i >= 2, 1.0, 0.0).astype(F32)
    vn = jnp.where(jnp.logical_and(i >= 1, i <= nt - 2), 1.0, 0.0).astype(F32)
    return vp, vn


def _dn_constants():
    r = np.arange(TM)[:, None]
    c = np.arange(TM)[None, :]
    same = (r // DN_CHUNK) == (c // DN_CHUNK)
    shifts = np.stack([c == r - 1, c == r + 1, c == r + 2]).astype(np.float32)
    sums = np.concatenate([same & (c <= r), same], axis=0).astype(np.float32)
    masks = np.stack([same & (c <= r), same & (c >= r), same & (c < r), same & (c > r),
                      (r // 16) == (c // 16), r == c]).astype(np.float32)
    return jnp.asarray(shifts, BF16), jnp.asarray(sums, BF16), jnp.asarray(masks, F32)


def _dn_chunk_kernel(x_ref, xp_ref, xn_ref, ab_ref, cw_ref, al_ref, dtb_ref, sh_ref, sm_ref, mk_ref,
                     wf_ref, uf_ref, qf_ref, kf_ref, qkf_ref, wb_ref, ub_ref, qb_ref, kb_ref, qkb_ref, ge_ref,
                     s_ref, gc_ref, bt_ref, gct_ref, gt_ref, *, nt):
    i = pl.program_id(1)
    vp, vn = _halo_valid(i, nt)
    row8 = lax.broadcasted_iota(jnp.int32, (8, HEAD_W), 0)
    for cb2 in range(3 * N_HEAD // 2):
        c2 = slice(cb2 * 2 * HEAD_W, (cb2 + 1) * 2 * HEAD_W)
        xb = x_ref[:, c2]
        xm1, xp1, xp2 = _dot(sh_ref[0], xb), _dot(sh_ref[1], xb), _dot(sh_ref[2], xb)
        xf = xb.astype(F32)
        for half in range(2):
            cb = cb2 * 2 + half
            cols = slice(cb * HEAD_W, (cb + 1) * HEAD_W)
            hc = slice(half * HEAD_W, (half + 1) * HEAD_W)
            w = cw_ref[:, cols]
            y = w[0:1, :] * xm1[:, hc] + w[1:2, :] * xf[:, hc] + w[2:3, :] * xp1[:, hc] + w[3:4, :] * xp2[:, hc]
            prev = xp_ref[:, cols].astype(F32)[15:16, :] * vp
            nxt = xn_ref[:, cols].astype(F32)
            n0, n1 = nxt[0:1, :] * vn, nxt[1:2, :] * vn
            top = jnp.where(row8 == 0, w[0:1, :] * prev, 0.0)
            bot = jnp.where(row8 == 6, w[3:4, :] * n0, jnp.where(row8 == 7, w[2:3, :] * n0 + w[3:4, :] * n1, 0.0))
            y = jnp.concatenate([y[0:8, :] + top, y[8:TM - 8, :], y[TM - 8:TM, :] + bot], axis=0)
            s = y * _sigmoid(y)
            if cb < 2 * N_HEAD:
                ss = jnp.sum(s * s, axis=-1, keepdims=True)
                s = s * (lax.rsqrt(ss + 1e-6) * (HEAD_W ** -0.5 if cb < N_HEAD else 1.0))
            s_ref[cb] = s

    ab = ab_ref[...]
    lane = lax.broadcasted_iota(jnp.int32, (TM, 128), 1)
    g = -jnp.exp(al_ref[...]) * _softplus(ab + dtb_ref[...])
    g1 = g.astype(BF16)
    r1 = g - g1.astype(F32)
    g2 = r1.astype(BF16)
    g3 = (r1 - g2.astype(F32)).astype(BF16)
    sums = _dot(sm_ref[...], jnp.concatenate([g1, g2, g3], axis=1))
    sums = sums[:, 0:128] + sums[:, 128:256] + sums[:, 256:384]
    g_f = sums[0:TM, :]
    gtot = sums[TM:2 * TM, :]
    gc = jnp.where(lane < N_HEAD, g_f, gtot - g_f + g)
    gc_ref[...] = gc
    bt_ref[...] = _sigmoid(ab)
    gt_ref[...] = gtot
    gct_ref[...] = gc.T
    ge_ref[...] = jnp.exp(gtot)
    out_refs = ((wf_ref, uf_ref, qf_ref, kf_ref, qkf_ref), (wb_ref, ub_ref, qb_ref, kb_ref, qkb_ref))
    sub8 = lax.broadcasted_iota(jnp.int32, (8, TM), 0)

    probs = [(h, d) for h in range(N_HEAD) for d in range(2)]
    kk, qk0 = {}, {}
    for h in range(N_HEAD):
        kb16 = s_ref[N_HEAD + h].astype(BF16)
        kk[h] = _dot_nt(kb16, kb16)
        qk0[h] = _dot_nt(s_ref[h].astype(BF16), kb16)
    tmat, dblk, lo_b, rhs_b = {}, {}, {}, {}
    for h, d in probs:
        q = s_ref[h]
        k = s_ref[N_HEAD + h]
        v = s_ref[2 * N_HEAD + h]
        idx = d * N_HEAD + h
        sel = lane == idx
        gc_col = jnp.sum(jnp.where(sel, gc_ref[...], 0.0), axis=1, keepdims=True)
        gt_col = jnp.sum(jnp.where(sel, gt_ref[...], 0.0), axis=1, keepdims=True)
        bt_col = jnp.sum(jnp.where(lane == idx + 2 * N_HEAD, bt_ref[...], 0.0), axis=1, keepdims=True)
        gc_row = jnp.sum(jnp.where(sub8 == idx, gct_ref[0:8, :], 0.0), axis=0, keepdims=True)
        e = jnp.exp(jnp.minimum(gc_col - gc_row, 0.0))
        a = kk[h] * (e * mk_ref[2 + d] * bt_col)
        eg = jnp.exp(gc_col)
        _, _, q_r, k_r, qk_r = out_refs[d]
        q_r[h] = (q * eg).astype(BF16)
        k_r[h] = (k * jnp.exp(gt_col - gc_col)).astype(BF16)
        qk_r[h] = (qk0[h] * (e * mk_ref[d])).astype(BF16)
        dm = a * mk_ref[4]
        tmat[h, d] = mk_ref[5] - dm
        lo_b[h, d] = (a - dm).astype(BF16)
        rhs_b[h, d] = jnp.concatenate([k * (bt_col * eg), v * bt_col], axis=1).astype(BF16)
        dblk[h, d] = dm.astype(BF16)
    pw = {pk: _dot(dblk[pk], dblk[pk]) for pk in probs}
    for m in range(3):
        for pk in probs:
            pb = pw[pk].astype(BF16)
            tmat[pk] = tmat[pk] + _dot(tmat[pk].astype(BF16), pb)
            if m < 2:
                pw[pk] = _dot(pb, pb)
    x0, mt, m2 = {}, {}, {}
    for pk in probs:
        tb = tmat[pk].astype(BF16)
        x0[pk] = _dot(tb, rhs_b[pk])
        mt[pk] = _dot(tb, lo_b[pk]).astype(BF16)
    for pk in probs:
        m2[pk] = _dot(mt[pk], mt[pk]).astype(BF16)
    ys = {pk: x0[pk] + _dot(m2[pk], x0[pk].astype(BF16)) for pk in probs}
    for h, d in probs:
        x = ys[h, d] - _dot(mt[h, d], ys[h, d].astype(BF16))
        w_r, u_r = out_refs[d][0], out_refs[d][1]
        w_r[h] = x[:, :HEAD_W].astype(BF16)
        u_r[h] = x[:, HEAD_W:]


def _dn_chunk_call(qkv, ab, conv_w, a_log, dt_bias):
    B, T, W = qkv.shape
    nt = T // TM
    hb = TM // 16
    last16 = T // 16 - 1
    al = jnp.zeros((1, 128), F32).at[0, :2 * N_HEAD].set(a_log.reshape(-1))
    dtb = jnp.zeros((1, 128), F32).at[0, :2 * N_HEAD].set(dt_bias.reshape(-1))
    shifts, sums, masks = _dn_constants()
    hspec = lambda w: pl.BlockSpec((None, N_HEAD, TM, w), lambda b, i: (b, 0, i, 0))
    hshape = lambda w, dt: jax.ShapeDtypeStruct((B, N_HEAD, T, w), dt)
    const = lambda a: pl.BlockSpec(a.shape, lambda b, i: (0,) * a.ndim)
    per_dir_specs = [hspec(HEAD_W), hspec(HEAD_W), hspec(HEAD_W), hspec(HEAD_W), hspec(TM)]
    per_dir_shapes = [hshape(HEAD_W, BF16), hshape(HEAD_W, F32), hshape(HEAD_W, BF16), hshape(HEAD_W, BF16),
                      hshape(TM, BF16)]
    return pl.pallas_call(
        functools.partial(_dn_chunk_kernel, nt=nt),
        grid=(B, nt),
        in_specs=[
            pl.BlockSpec((None, TM, W), lambda b, i: (b, i, 0)),
            pl.BlockSpec((None, 16, W), lambda b, i: (b, jnp.maximum(i * hb - 1, 0), 0)),
            pl.BlockSpec((None, 16, W), lambda b, i: (b, jnp.minimum((i + 1) * hb, last16), 0)),
            pl.BlockSpec((None, TM, 128), lambda b, i: (b, i, 0)),
            pl.BlockSpec((4, W), lambda b, i: (0, 0)),
            pl.BlockSpec((1, 128), lambda b, i: (0, 0)),
            pl.BlockSpec((1, 128), lambda b, i: (0, 0)),
            const(shifts), const(sums), const(masks),
        ],
        out_specs=per_dir_specs + per_dir_specs + [pl.BlockSpec((None, TM, 128), lambda b, i: (b, i, 0))],
        out_shape=per_dir_shapes + per_dir_shapes + [jax.ShapeDtypeStruct((B, T, 128), F32)],
        scratch_shapes=[
            pltpu.VMEM((3 * N_HEAD, TM, HEAD_W), F32),
            pltpu.VMEM((TM, 128), F32), pltpu.VMEM((TM, 128), F32),
            pltpu.VMEM((128, TM), F32), pltpu.VMEM((TM, 128), F32),
        ],
        compiler_params=_cparams(("arbitrary", "arbitrary")),
        name="dn_chunk",
    )(qkv, qkv, qkv, ab, conv_w, al, dtb, shifts, sums, masks)


def _dn_seq_kernel(wf_ref, uf_ref, qf_ref, kf_ref, qkf_ref, gef_ref, wb_ref, ub_ref, qb_ref, kb_ref, qkb_ref,
                   geb_ref, of_ref, ob_ref, st_ref):
    C = DN_CHUNK
    nc = TM // C

    @pl.when(pl.program_id(1) == 0)
    def _():
        st_ref[...] = jnp.zeros_like(st_ref)

    ins = ((wf_ref, uf_ref, qf_ref, kf_ref, qkf_ref, gef_ref, of_ref),
           (wb_ref, ub_ref, qb_ref, kb_ref, qkb_ref, geb_ref, ob_ref))
    chains = [(d, bb, h) for d in range(2) for bb in range(DN_SEQ_BB) for h in range(N_HEAD)]
    for ci in range(nc):
        st, sb, ws, qs, vb = {}, {}, {}, {}, {}
        for ch in chains:
            d, bb, h = ch
            c = ci if d == 0 else nc - 1 - ci
            rows = slice(c * C, (c + 1) * C)
            st[ch] = st_ref[(d * DN_SEQ_BB + bb) * N_HEAD + h]
            sb[ch] = st[ch].astype(BF16)
            ws[ch] = _dot(ins[d][0][bb, h, rows, :], sb[ch])
            qs[ch] = _dot(ins[d][2][bb, h, rows, :], sb[ch])
        for ch in chains:
            d, bb, h = ch
            c = ci if d == 0 else nc - 1 - ci
            rows = slice(c * C, (c + 1) * C)
            vb[ch] = (ins[d][1][bb, h, rows, :] - ws[ch]).astype(BF16)
        for ch in chains:
            d, bb, h = ch
            c = ci if d == 0 else nc - 1 - ci
            rows = slice(c * C, (c + 1) * C)
            _, _, _, k_r, qk_r, ge_r, o_r = ins[d]
            si = d * N_HEAD + h
            ge = ge_r[bb, c * C:c * C + 8, :][0:1, si:si + 1]
            o_r[bb, rows, h * HEAD_W:(h + 1) * HEAD_W] = qs[ch] + _dot(qk_r[bb, h, rows, c * C:(c + 1) * C], vb[ch])
            st_ref[(d * DN_SEQ_BB + bb) * N_HEAD + h] = ge * st[ch] + _dot_tn(k_r[bb, h, rows, :], vb[ch])


def _dn_seq_call(chunk_outs):
    wf, uf, qf, kf, qkf, wb, ub, qb, kb, qkb, ge = chunk_outs
    B, _, T, _ = wf.shape
    nt = T // TM
    bb = DN_SEQ_BB
    tf = lambda s: s
    tb = lambda s: jnp.where(s == 0, 0, nt - s)
    hspec = lambda w, t: pl.BlockSpec((bb, N_HEAD, TM, w), lambda b, s: (b, 0, t(s), 0))
    gspec = lambda t: pl.BlockSpec((bb, TM, 128), lambda b, s: (b, t(s), 0))
    dir_specs = lambda t: [hspec(HEAD_W, t), hspec(HEAD_W, t), hspec(HEAD_W, t), hspec(HEAD_W, t), hspec(TM, t),
                           gspec(t)]
    ospec = lambda t: pl.BlockSpec((bb, TM, MIX_W), lambda b, s: (b, t(s), 0))
    return pl.pallas_call(
        _dn_seq_kernel,
        grid=(B // bb, nt),
        in_specs=dir_specs(tf) + dir_specs(tb),
        out_specs=[ospec(tf), ospec(tb)],
        out_shape=[jax.ShapeDtypeStruct((B, T, MIX_W), F32)] * 2,
        scratch_shapes=[pltpu.VMEM((2 * bb * N_HEAD, HEAD_W, HEAD_W), F32)],
        compiler_params=_cparams(("arbitrary", "arbitrary")),
        name="dn_seq",
    )(wf, uf, qf, kf, qkf, ge, wb, ub, qb, kb, qkb, ge)


def _gelu_tanh(x):
    return 0.5 * x * (1.0 + jnp.tanh(math.sqrt(2.0 / math.pi) * (x + 0.044715 * (x * x * x))))


def _lru_kernel(x_ref, y_ref, cw_ref, cb_ref, wg_ref, bg_ref, lam_ref, sh_ref, o_ref, hf_ref, xc_ref, *, nt):
    CW = x_ref.shape[1]
    T = x_ref.shape[0]
    G = TM // 8
    row8 = lax.broadcasted_iota(jnp.int32, (8, CW), 0)
    sub = lax.broadcasted_iota(jnp.int32, (G, 8, CW), 1)

    def conv(t):
        base = pl.multiple_of(t * TM, TM)
        xb = x_ref[pl.ds(base, TM), :]
        xp = x_ref[pl.ds(pl.multiple_of(jnp.maximum(base - 16, 0), 16), 16), :].astype(F32)
        xn = x_ref[pl.ds(pl.multiple_of(jnp.minimum(base + TM, T - 16), 16), 16), :].astype(F32)
        vp, vn = _halo_valid(t, nt)
        w = cw_ref[...]
        y = (w[0:1, :] * _dot(sh_ref[0], xb) + w[1:2, :] * xb.astype(F32)
             + w[2:3, :] * _dot(sh_ref[1], xb) + w[3:4, :] * _dot(sh_ref[2], xb))
        n0, n1 = xn[0:1, :] * vn, xn[1:2, :] * vn
        top = jnp.where(row8 == 0, w[0:1, :] * (xp[15:16, :] * vp), 0.0)
        bot = jnp.where(row8 == 6, w[3:4, :] * n0, jnp.where(row8 == 7, w[2:3, :] * n0 + w[3:4, :] * n1, 0.0))
        return jnp.concatenate([y[0:8, :] + top, y[8:TM - 8, :], y[TM - 8:TM, :] + bot], axis=0) + cb_ref[...]

    def coeffs(xc, d):
        z = _dot(xc.astype(BF16), wg_ref[:, d * 2 * CW:(d + 1) * 2 * CW]) + bg_ref[:, d * 2 * CW:(d + 1) * 2 * CW]
        rg = _sigmoid(z[:, :CW])
        ig = _sigmoid(z[:, CW:])
        log_a = (-LRU_C * _softplus(-lam_ref[d:d + 1, :])) * rg
        a = jnp.exp(log_a)
        om = 1.0 - a * a
        b = jnp.where(om > 0.0, om * lax.rsqrt(om), 0.0) * (ig * xc)
        return a, b

    def scan_tile(a, b, h_in, d):
        a = a.reshape(G, 8, CW)
        b = b.reshape(G, 8, CW)
        for s in (1, 2, 4):
            if d == 0:
                a_s, b_s, valid = pltpu.roll(a, s, 1), pltpu.roll(b, s, 1), sub >= s
            else:
                a_s, b_s, valid = pltpu.roll(a, 8 - s, 1), pltpu.roll(b, 8 - s, 1), sub < 8 - s
            b = jnp.where(valid, a * b_s + b, b)
            a = jnp.where(valid, a * a_s, a)
        hs = [None] * G
        h = h_in
        for g in (range(G) if d == 0 else range(G - 1, -1, -1)):
            hg = b[g] + a[g] * h
            hs[g] = hg
            h = hg[7:8, :] if d == 0 else hg[0:1, :]
        return jnp.concatenate(hs, axis=0), h

    def fwd(t, h):
        rows = pl.ds(pl.multiple_of(t * TM, TM), TM)
        xc = conv(t)
        xc_ref[rows, :] = xc
        a, b = coeffs(xc, 0)
        hf, h = scan_tile(a, b, h, 0)
        hf_ref[rows, :] = hf
        return h

    lax.fori_loop(0, nt, fwd, jnp.zeros((1, CW), F32))

    def bwd(s, h):
        t = jnp.where(s == 0, 0, nt - s)
        rows = pl.ds(pl.multiple_of(t * TM, TM), TM)
        a, b = coeffs(xc_ref[rows, :], 1)
        hb, h = scan_tile(a, b, h, 1)
        y = y_ref[rows, :].astype(F32)
        o_ref[rows, :] = ((hf_ref[rows, :] + hb) * _gelu_tanh(y)).astype(o_ref.dtype)
        return h

    lax.fori_loop(0, nt, bwd, jnp.zeros((1, CW), F32))


def _lru_call(x, y, conv_w, conv_b, wg, bg, lam):
    B, T, W = x.shape
    cw = W // 2
    nt = T // TM
    shifts = _dn_constants()[0]
    return pl.pallas_call(
        functools.partial(_lru_kernel, nt=nt),
        grid=(B, 2),
        in_specs=[
            pl.BlockSpec((None, T, cw), lambda b, p: (b, 0, p)),
            pl.BlockSpec((None, T, cw), lambda b, p: (b, 0, p)),
            pl.BlockSpec((4, cw), lambda b, p: (0, p)),
            pl.BlockSpec((1, cw), lambda b, p: (0, p)),
            pl.BlockSpec((None, cw, 4 * cw), lambda b, p: (p, 0, 0)),
            pl.BlockSpec((None, 1, 4 * cw), lambda b, p: (p, 0, 0)),
            pl.BlockSpec((2, cw), lambda b, p: (0, p)),
            pl.BlockSpec(shifts.shape, lambda b, p: (0, 0, 0)),
        ],
        out_specs=pl.BlockSpec((None, T, cw), lambda b, p: (b, 0, p)),
        out_shape=jax.ShapeDtypeStruct((B, T, W), BF16),
        scratch_shapes=[pltpu.VMEM((T, cw), F32), pltpu.VMEM((T, cw), F32)],
        compiler_params=_cparams(("arbitrary", "arbitrary")),
        name="rglru",
    )(x, y, conv_w, conv_b, wg, bg, lam, shifts)


def _rope(x, cs, sn, first_half):
    swapped = jnp.where(first_half, pltpu.roll(x, 112, 1), pltpu.roll(x, 16, 1))
    return x * cs + swapped * sn


def _attn_kernel(*refs, lam_init, n_q, nt):
    q_refs = refs[:n_q]
    k_ref, v_ref = refs[n_q:n_q + 2]
    tab_refs = refs[n_q + 2:3 * n_q + 2]
    ck_ref, sk_ref, lv_ref, nw_ref, o_ref, kr_ref, vt_ref, qt_ref, st0_ref, st1_ref, acc_ref = refs[3 * n_q + 2:]
    lane = lax.broadcasted_iota(jnp.int32, (TM, HEAD_W), 1)
    first_half = (lane % 32) < 16
    heads = [(hh, slice(hh * HEAD_W, (hh + 1) * HEAD_W)) for hh in range(ATTN_HP)]
    chains = [(qi, hh) for qi in range(n_q) for hh in range(ATTN_HP)]

    @pl.when(pl.program_id(2) == 0)
    def _():
        extra = (lax.broadcasted_iota(jnp.int32, (VT_ROWS - HEAD_W, TM), 0) == 0).astype(BF16)

        def prep_kv(t, carry):
            rows = pl.ds(pl.multiple_of(t * TM, TM), TM)
            for hh, cols in heads:
                kr_ref[hh, rows, :] = _rope(k_ref[rows, cols].astype(F32), ck_ref[rows, :], sk_ref[rows, :],
                                            first_half).astype(BF16)
                vt_ref[hh, 0:HEAD_W, rows] = v_ref[rows, cols].astype(F32).T.astype(BF16)
                vt_ref[hh, HEAD_W:VT_ROWS, rows] = extra
            return carry
        lax.fori_loop(0, nt, prep_kv, 0)

    for qi, hh in chains:
        q = _rope(q_refs[qi][:, heads[hh][1]].astype(F32), tab_refs[2 * qi][...], tab_refs[2 * qi + 1][...],
                  first_half) * (64 ** -0.5 * math.log2(math.e))
        qt_ref[qi * ATTN_HP + hh] = jnp.concatenate(
            [jnp.where(lane < 64, q, 0.0).T, jnp.where(lane >= 64, q, 0.0).T], axis=1).astype(BF16)
        acc_ref[qi * ATTN_HP + hh] = jnp.zeros((VT_ROWS, 2 * TM), F32)

    blocks = [slice(0, TM)] + [slice(s0 * TM, min(s0 + ATTN_KB, nt) * TM) for s0 in range(1, nt, ATTN_KB)]

    def scores(keys, st_ref):
        nk = keys.stop - keys.start
        mx = {}
        for qi, hh in chains:
            ci = qi * ATTN_HP + hh
            st = _dot(kr_ref[hh, keys, :], qt_ref[ci])
            st_ref[ci, 0:nk, :] = st
            mx[ci] = jnp.max(st, axis=0, keepdims=True)
        return mx

    def softmax_pv(keys, st_ref, ms, mx):
        nk = keys.stop - keys.start
        out = {}
        for qi, hh in chains:
            ci = qi * ATTN_HP + hh
            m_new = jnp.maximum(ms[ci], mx[ci])
            alpha = jnp.exp2(ms[ci] - m_new)
            for half in range(2):
                hc = slice(half * TM, (half + 1) * TM)
                p = jnp.exp2(st_ref[ci, 0:nk, hc] - m_new[:, hc]).astype(BF16)
                acc_ref[ci, :, hc] = alpha[:, hc] * acc_ref[ci, :, hc] + _dot(vt_ref[hh, :, keys], p)
            out[ci] = m_new
        return out

    bufs = (st0_ref, st1_ref)
    mx = scores(blocks[0], st0_ref)
    ms = {qi * ATTN_HP + hh: jnp.full((1, 2 * TM), -1e30, F32) for qi, hh in chains}
    for j, keys in enumerate(blocks):
        mx_next = scores(blocks[j + 1], bufs[(j + 1) % 2]) if j + 1 < len(blocks) else None
        ms = softmax_pv(keys, bufs[j % 2], ms, mx)
        mx = mx_next
    lv = lv_ref[...]
    lam = (jnp.exp(jnp.sum(lv[0:1, :] * lv[1:2, :], axis=-1, keepdims=True))
           - jnp.exp(jnp.sum(lv[2:3, :] * lv[3:4, :], axis=-1, keepdims=True)) + lam_init)
    for qi, hh in chains:
        ci = qi * ATTN_HP + hh
        on = acc_ref[ci, 0:HEAD_W, :] * (1.0 / acc_ref[ci, HEAD_W:HEAD_W + 1, :])
        o = (on[:, :TM] - lam * on[:, TM:]).T
        ms_o = jnp.mean(o * o, axis=-1, keepdims=True)
        o_ref[qi * TM:(qi + 1) * TM, heads[hh][1]] = (
            o * lax.rsqrt(ms_o + 1e-5) * nw_ref[...] * (1.0 - lam_init)).astype(o_ref.dtype)


def _attn_call(qkv, cos_t, sin_t, lam_vecs, sub_norm, lam_init, ctx):
    B, T, _ = qkv.shape
    pw = ATTN_HP * HEAD_W
    npair = N_HEAD // ATTN_HP
    if ctx:
        n_q, nk, nsteps, q_tiles = 1, 1, 1, [lambda s: 0]
    else:
        assert (T // TM - 1) % 2 == 0, "latent query tiles are processed in pairs"
        n_q, nk, nsteps = 2, T // TM, (T // TM - 1) // 2
        q_tiles = [lambda s: 2 * s + 1, lambda s: 2 * s + 2]
    nch = n_q * ATTN_HP
    qspec = lambda t: pl.BlockSpec((None, TM, pw), lambda b, h, s: (b, t(s), h))
    tspec = lambda t: pl.BlockSpec((TM, HEAD_W), lambda b, h, s: (t(s), 0))
    tabs = [tspec(t) for t in q_tiles for _ in range(2)]
    kern = functools.partial(_attn_kernel, lam_init=lam_init, n_q=n_q, nt=nk)
    return pl.pallas_call(
        kern,
        grid=(B, npair, nsteps),
        in_specs=[qspec(t) for t in q_tiles] + [
            pl.BlockSpec((None, nk * TM, pw), lambda b, h, s: (b, 0, npair + h)),
            pl.BlockSpec((None, nk * TM, pw), lambda b, h, s: (b, 0, 2 * npair + h)),
        ] + tabs + [
            pl.BlockSpec((nk * TM, HEAD_W), lambda b, h, s: (0, 0)),
            pl.BlockSpec((nk * TM, HEAD_W), lambda b, h, s: (0, 0)),
            pl.BlockSpec((4, 64), lambda b, h, s: (0, 0)),
            pl.BlockSpec((1, HEAD_W), lambda b, h, s: (0, 0)),
        ],
        out_specs=pl.BlockSpec((None, n_q * TM, pw), lambda b, h, s: (b, s, h)),
        out_shape=jax.ShapeDtypeStruct((B, nsteps * n_q * TM, MIX_W), BF16),
        scratch_shapes=[pltpu.VMEM((ATTN_HP, nk * TM, HEAD_W), BF16), pltpu.VMEM((ATTN_HP, VT_ROWS, nk * TM), BF16),
                        pltpu.VMEM((nch, HEAD_W, 2 * TM), BF16),
                        pltpu.VMEM((nch, ATTN_KB * TM, 2 * TM), F32), pltpu.VMEM((nch, ATTN_KB * TM, 2 * TM), F32),
                        pltpu.VMEM((nch, VT_ROWS, 2 * TM), F32)],
        compiler_params=_cparams(("arbitrary", "arbitrary", "arbitrary")),
        name="diff_attn_ctx" if ctx else "diff_attn",
    )(*([qkv] * (n_q + 2)), *([cos_t, sin_t] * n_q), cos_t, sin_t, lam_vecs, sub_norm)


def _mix_ffn_kernel(*refs, n_h, t_off, final):
    mod_ref, dnf_ref, dnb_ref, z_ref, yb_ref = refs[n_h:n_h + 5]
    rest = refs[n_h + 5:]
    (ycl_ref, g_ref, dnw_ref, wb_ref, wo_ref, nw_ref, wg_ref, wu_ref, wd_ref, fw_ref, o_ref) = rest[-11:]
    yc = ycl_ref[...]
    if len(rest) == 12:
        yc = jnp.where(pl.program_id(1) + t_off == 0, rest[0][...], yc)
    dnw = dnw_ref[...]
    ya_parts = []
    for hh in range(N_HEAD):
        cols = slice(hh * HEAD_W, (hh + 1) * HEAD_W)
        o = dnf_ref[:, cols] + dnb_ref[:, cols]
        ms = jnp.mean(o * o, axis=-1, keepdims=True)
        z = z_ref[:, cols].astype(F32)
        ya_parts.append(((o * lax.rsqrt(ms + 1e-6) * dnw) * (z * _sigmoid(z))).astype(BF16))
    ys = (jnp.concatenate(ya_parts, axis=1), yb_ref[...], yc)
    mix = None
    for n in range(3):
        up = _dot(ys[n], wb_ref[n])
        term = _sigmoid(g_ref[:, n * D_MODEL:(n + 1) * D_MODEL].astype(F32)) * up
        mix = term if mix is None else mix + term
    h = _read_stream(refs[:n_h]) + mod_ref[2:3, :] * _dot(mix.astype(BF16), wo_ref[...])

    x = _norm_modulate(h, nw_ref[...], mod_ref[3:4, :], mod_ref[4:5, :]).astype(BF16)
    acc = None
    for c in range(0, D_FF, FF_CHUNK):
        ce = min(c + FF_CHUNK, D_FF)
        g = _dot(x, wg_ref[:, c:ce])
        u = _dot(x, wu_ref[:, c:ce])
        part = _dot(((g * _sigmoid(g)) * u).astype(BF16), wd_ref[c:ce, :])
        acc = part if acc is None else acc + part
    out = h + mod_ref[5:6, :] * acc
    if final:
        ms = jnp.mean(out * out, axis=-1, keepdims=True)
        out = out * lax.rsqrt(ms + 1e-6) * fw_ref[...]
    o_ref[...] = out


def _mix_ffn_call(hs, mod, dnf, dnb, z, yb, yc_ctx, yc_lat, gates, dn_norm, wb, wo, nw, wg, wu, wd, fw, t_off, final):
    B = hs[0].shape[0]
    nt = sum(h.shape[1] for h in hs) // TM - t_off
    row = lambda w: pl.BlockSpec((None, TM, w), lambda b, i: (b, i + t_off, 0))
    const = lambda a: pl.BlockSpec(a.shape, lambda b, i: (0,) * a.ndim, pipeline_mode=pl.Buffered(1))
    yc_specs = [pl.BlockSpec((None, TM, MIX_W), lambda b, i: (b, jnp.maximum(i + t_off - 1, 0), 0))]
    yc_args = [yc_lat]
    if yc_ctx is not None:
        yc_specs.insert(0, pl.BlockSpec((None, TM, MIX_W), lambda b, i: (b, 0, 0)))
        yc_args.insert(0, yc_ctx)
    return pl.pallas_call(
        functools.partial(_mix_ffn_kernel, n_h=len(hs), t_off=t_off, final=final),
        grid=(B, nt),
        in_specs=_stream_specs(hs, t_off) + [
            pl.BlockSpec((None, None, 6, D_MODEL), lambda b, i: _mod_index(b, i + t_off)),
            row(MIX_W), row(MIX_W), row(MIX_W), row(MIX_W), *yc_specs, row(3 * D_MODEL),
            const(dn_norm), const(wb), const(wo), const(nw), const(wg), const(wu), const(wd), const(fw),
        ],
        out_specs=pl.BlockSpec((None, TM, D_MODEL), lambda b, i: (b, i, 0)),
        out_shape=jax.ShapeDtypeStruct((B, nt * TM, D_MODEL), F32),
        compiler_params=_cparams(("arbitrary", "arbitrary")),
        name="mix_ffn",
    )(*hs, mod, dnf, dnb, z, yb, *yc_args, gates, dn_norm, wb, wo, nw, wg, wu, wd, fw)


def _reorder_w_in(w):
    parts = [w[:, 0:1536], w[:, 1536:2048], w[:, 2064:2576], w[:, 2576:3088], w[:, 3088:4624], w[:, 4624:7696],
             w[:, 2048:2064], jnp.zeros((w.shape[0], IN_COLS_PAD - 7696), w.dtype)]
    return jnp.concatenate(parts, axis=1).astype(BF16)


def _block_diag(w):
    g, n, _ = w.shape
    return jnp.einsum('gij,gh->gihj', w, jnp.eye(g, dtype=w.dtype)).reshape(g * n, g * n)


def _lru_gate_weights(wa, ba, wi, bi):
    half = LRU_G // 2
    ws, bs = [], []
    for p in range(2):
        grp = slice(p * half, (p + 1) * half)
        ch = slice(p * MIX_W // 2, (p + 1) * MIX_W // 2)
        cols, bcols = [], []
        for d in range(2):
            cols += [_block_diag(wa[d, grp]), _block_diag(wi[d, grp])]
            bcols += [ba[d, ch], bi[d, ch]]
        ws.append(jnp.concatenate(cols, axis=1))
        bs.append(jnp.concatenate(bcols)[None, :])
    return jnp.stack(ws).astype(BF16), jnp.stack(bs)


def _rope_tables(seq):
    rows = seq // GRID_W
    rowp = jnp.repeat(jnp.arange(rows, dtype=F32), GRID_W)
    colp = jnp.tile(jnp.arange(GRID_W, dtype=F32), rows)
    inv = ROPE_BASE ** (-jnp.arange(ROPE_NF, dtype=F32) / ROPE_NF)
    ar, ac = rowp[:, None] * inv, colp[:, None] * inv
    cs = jnp.concatenate([jnp.cos(ar), jnp.cos(ar), jnp.cos(ac), jnp.cos(ac)], axis=1)
    sn = jnp.concatenate([-jnp.sin(ar), jnp.sin(ar), -jnp.sin(ac), jnp.sin(ac)], axis=1)
    cs = jnp.concatenate([jnp.ones((CTX_LEN, 64), F32), cs], axis=0)
    sn = jnp.concatenate([jnp.zeros((CTX_LEN, 64), F32), sn], axis=0)
    return jnp.tile(cs, (1, 2)), jnp.tile(sn, (1, 2))


def kernel(x, c, ctx, c_ctx, w_mod, b_mod, norm_mix, norm_ffn, w_in, dn_conv, dn_a_log, dn_dt_bias, dn_norm, lru_conv_w, lru_conv_b, lru_wa, lru_ba, lru_wi, lru_bi, lru_lambda, da_lambda, da_norm, w_branch, w_out, w_ffn_gate, w_ffn_up, w_ffn_down, norm_final):
    B, S, D = x.shape
    depth = w_mod.shape[0]
    rows = jnp.concatenate([c, c_ctx[None, :], jnp.zeros((16 - B - 1, D), F32)], axis=0)
    mod = _mod_call(rows, w_mod, b_mod).reshape(depth, 16, 6, D)
    cos_t, sin_t = _rope_tables(S)
    hs = (ctx, x)
    for l in range(depth):
        last = l == depth - 1
        t_off = 1 if last else 0
        lam_init = 0.8 - 0.6 * math.exp(-0.3 * l)
        mod_l = jnp.stack([jnp.broadcast_to(mod[l, B][None], (B, 6, D)), mod[l, :B]], axis=1)
        dn_qkv, dn_z, lru_x, lru_y, da_qkv, gates, ab = _inproj_call(
            hs, mod_l, norm_mix[l][None, :], _reorder_w_in(w_in[l]))
        dn_f, dn_b = _dn_seq_call(_dn_chunk_call(dn_qkv, ab, dn_conv[l], dn_a_log[l], dn_dt_bias[l]))
        wg, bg = _lru_gate_weights(lru_wa[l], lru_ba[l], lru_wi[l], lru_bi[l])
        yb = _lru_call(lru_x, lru_y, lru_conv_w[l], lru_conv_b[l][None, :], wg, bg, lru_lambda[l])
        attn = functools.partial(_attn_call, da_qkv, cos_t, sin_t, da_lambda[l], da_norm[l][None, :], lam_init)
        yc_ctx, yc_lat = (None if last else attn(True)), attn(False)
        hs = (_mix_ffn_call(hs, mod_l, dn_f, dn_b, dn_z, yb, yc_ctx, yc_lat, gates, dn_norm[l][None, :],
                            w_branch[l].astype(BF16), w_out[l].astype(BF16), norm_ffn[l][None, :],
                            w_ffn_gate[l].astype(BF16), w_ffn_up[l].astype(BF16), w_ffn_down[l].astype(BF16),
                            norm_final[None, :], t_off, last),)
    return hs[0]
```

```python
import functools
import math

import jax
import jax.numpy as jnp
import numpy as np
from jax import lax
from jax.experimental import pallas as pl
from jax.experimental.pallas import tpu as pltpu

F32 = jnp.float32
BF16 = jnp.bfloat16

D_MODEL = 1024
CTX_LEN = 256
GRID_W = 64
MIX_W = 512
N_HEAD = 4
HEAD_W = 128
DN_CHUNK = 64
LRU_C = 8.0
LRU_G = 8
ROPE_BASE = 10000.0
ROPE_NF = 16
D_FF = 2816
TM = 256
FF_CHUNK = 1024
VT_ROWS = HEAD_W + 16
ATTN_HP = 2
ATTN_KB = 4
DN_SEQ_BB = 4
VMEM_LIMIT = 56 * 1024 * 1024

C_DNQKV, C_DNZ, C_LRUX, C_LRUY, C_DAQKV, C_GATE, C_AB = 0, 1536, 2048, 2560, 3072, 4608, 7680
IN_COLS_PAD = 7808


def _cparams(sem):
    return pltpu.CompilerParams(dimension_semantics=sem, vmem_limit_bytes=VMEM_LIMIT)


def _sigmoid(x):
    return 1.0 / (1.0 + jnp.exp(-x))


def _softplus(x):
    return jnp.maximum(x, 0.0) + jnp.log(1.0 + jnp.exp(-jnp.abs(x)))


def _dot(a, b):
    return jnp.dot(a, b, preferred_element_type=F32)


def _dot_nt(a, b):
    return lax.dot_general(a, b, (((1,), (1,)), ((), ())), preferred_element_type=F32)


def _dot_tn(a, b):
    return lax.dot_general(a, b, (((0,), (0,)), ((), ())), preferred_element_type=F32)


def _mod_kernel(s_ref, w_ref, b_ref, o_ref):
    s = s_ref[...]
    s = s * _sigmoid(s)
    o_ref[...] = jnp.dot(s, w_ref[...], preferred_element_type=F32,
                         precision=lax.Precision.HIGHEST) + b_ref[...]


def _mod_call(rows, w_mod, b_mod):
    depth = w_mod.shape[0]
    n6 = w_mod.shape[2]
    tn = 1024
    return pl.pallas_call(
        _mod_kernel,
        grid=(depth, n6 // tn),
        in_specs=[
            pl.BlockSpec((16, D_MODEL), lambda l, j: (0, 0)),
            pl.BlockSpec((None, D_MODEL, tn), lambda l, j: (l, 0, j)),
            pl.BlockSpec((None, 1, tn), lambda l, j: (l, 0, j)),
        ],
        out_specs=pl.BlockSpec((None, 16, tn), lambda l, j: (l, 0, j)),
        out_shape=jax.ShapeDtypeStruct((depth, 16, n6), F32),
        compiler_params=_cparams(("arbitrary", "arbitrary")),
        name="adaln_mod",
    )(rows, w_mod, b_mod.reshape(depth, 1, n6))


def _mod_index(b, i):
    return (b, jnp.minimum(i, 1), 0, 0)


def _stream_specs(parts, t_off):
    if len(parts) == 1:
        return [pl.BlockSpec((None, TM, D_MODEL), lambda b, i: (b, i + t_off, 0))]
    assert t_off == 0
    return [pl.BlockSpec((None, TM, D_MODEL), lambda b, i: (b, 0, 0)),
            pl.BlockSpec((None, TM, D_MODEL), lambda b, i: (b, jnp.maximum(i - 1, 0), 0))]


def _read_stream(refs):
    if len(refs) == 1:
        return refs[0][...]
    return jnp.where(pl.program_id(1) == 0, refs[0][...], refs[1][...])


def _norm_modulate(x, nw, shift, scale, eps=1e-6):
    ms = jnp.mean(x * x, axis=-1, keepdims=True)
    y = x * lax.rsqrt(ms + eps) * nw
    return y * (1.0 + scale) + shift


_INPROJ_SEGS = (
    (C_DNQKV, 1536), (C_DNZ, 512), (C_LRUX, 512), (C_LRUY, 512), (C_DAQKV, 1536), (C_GATE, 3072), (C_AB, 128))


def _inproj_kernel(*refs, n_h):
    mod_ref, nw_ref, w_ref = refs[n_h:n_h + 3]
    out_refs = refs[n_h + 3:]
    u = _norm_modulate(_read_stream(refs[:n_h]), nw_ref[...], mod_ref[0:1, :], mod_ref[1:2, :]).astype(BF16)
    for (c0, width), o_ref in zip(_INPROJ_SEGS, out_refs):
        step = min(width, 512)
        for c in range(0, width, step):
            o_ref[:, c:c + step] = _dot(u, w_ref[:, c0 + c:c0 + c + step]).astype(o_ref.dtype)


def _inproj_call(hs, mod, nw, w):
    B = hs[0].shape[0]
    T = sum(h.shape[1] for h in hs)
    nt = T // TM
    dts = (BF16, BF16, BF16, BF16, BF16, BF16, F32)
    return pl.pallas_call(
        functools.partial(_inproj_kernel, n_h=len(hs)),
        grid=(B, nt),
        in_specs=_stream_specs(hs, 0) + [
            pl.BlockSpec((None, None, 6, D_MODEL), _mod_index),
            pl.BlockSpec((1, D_MODEL), lambda b, i: (0, 0)),
            pl.BlockSpec((D_MODEL, IN_COLS_PAD), lambda b, i: (0, 0)),
        ],
        out_specs=[pl.BlockSpec((None, TM, wd), lambda b, i: (b, i, 0)) for _, wd in _INPROJ_SEGS],
        out_shape=[jax.ShapeDtypeStruct((B, T, wd), dt) for (_, wd), dt in zip(_INPROJ_SEGS, dts)],
        compiler_params=_cparams(("arbitrary", "arbitrary")),
        name="in_proj",
    )(*hs, mod, nw, w)


def _halo_valid(i, nt):
    vp = jnp.where(i >= 2, 1.0, 0.0).astype(F32)
    vn = jnp.where(jnp.logical_and(i >= 1, i <= nt - 2), 1.0, 0.0).astype(F32)
    return vp, vn


def _dn_constants():
    r = np.arange(TM)[:, None]
    c = np.arange(TM)[None, :]
    same = (r // DN_CHUNK) == (c // DN_CHUNK)
    shifts = np.stack([c == r - 1, c == r + 1, c == r + 2]).astype(np.float32)
    sums = np.concatenate([same & (c <= r), same], axis=0).astype(np.float32)
    masks = np.stack([same & (c <= r), same & (c >= r), same & (c < r), same & (c > r),
                      (r // 16) == (c // 16), r == c]).astype(np.float32)
    return jnp.asarray(shifts, BF16), jnp.asarray(sums, BF16), jnp.asarray(masks, F32)


def _dn_chunk_kernel(x_ref, xp_ref, xn_ref, ab_ref, cw_ref, al_ref, dtb_ref, sh_ref, sm_ref, mk_ref,
                     wf_ref, uf_ref, qf_ref, kf_ref, qkf_ref, wb_ref, ub_ref, qb_ref, kb_ref, qkb_ref, ge_ref,
                     s_ref, gc_ref, bt_ref, gct_ref, gt_ref, *, nt):
    i = pl.program_id(1)
    vp, vn = _halo_valid(i, nt)
    row8 = lax.broadcasted_iota(jnp.int32, (8, HEAD_W), 0)
    for cb2 in range(3 * N_HEAD // 2):
        c2 = slice(cb2 * 2 * HEAD_W, (cb2 + 1) * 2 * HEAD_W)
        xb = x_ref[:, c2]
        xm1, xp1, xp2 = _dot(sh_ref[0], xb), _dot(sh_ref[1], xb), _dot(sh_ref[2], xb)
        xf = xb.astype(F32)
        for half in range(2):
            cb = cb2 * 2 + half
            cols = slice(cb * HEAD_W, (cb + 1) * HEAD_W)
            hc = slice(half * HEAD_W, (half + 1) * HEAD_W)
            w = cw_ref[:, cols]
            y = w[0:1, :] * xm1[:, hc] + w[1:2, :] * xf[:, hc] + w[2:3, :] * xp1[:, hc] + w[3:4, :] * xp2[:, hc]
            prev = xp_ref[:, cols].astype(F32)[15:16, :] * vp
            nxt = xn_ref[:, cols].astype(F32)
            n0, n1 = nxt[0:1, :] * vn, nxt[1:2, :] * vn
            top = jnp.where(row8 == 0, w[0:1, :] * prev, 0.0)
            bot = jnp.where(row8 == 6, w[3:4, :] * n0, jnp.where(row8 == 7, w[2:3, :] * n0 + w[3:4, :] * n1, 0.0))
            y = jnp.concatenate([y[0:8, :] + top, y[8:TM - 8, :], y[TM - 8:TM, :] + bot], axis=0)
            s = y * _sigmoid(y)
            if cb < 2 * N_HEAD:
                ss = jnp.sum(s * s, axis=-1, keepdims=True)
                s = s * (lax.rsqrt(ss + 1e-6) * (HEAD_W ** -0.5 if cb < N_HEAD else 1.0))
            s_ref[cb] = s

    ab = ab_ref[...]
    lane = lax.broadcasted_iota(jnp.int32, (TM, 128), 1)
    g = -jnp.exp(al_ref[...]) * _softplus(ab + dtb_ref[...])
    g1 = g.astype(BF16)
    r1 = g - g1.astype(F32)
    g2 = r1.astype(BF16)
    g3 = (r1 - g2.astype(F32)).astype(BF16)
    sums = _dot(sm_ref[...], jnp.concatenate([g1, g2, g3], axis=1))
    sums = sums[:, 0:128] + sums[:, 128:256] + sums[:, 256:384]
    g_f = sums[0:TM, :]
    gtot = sums[TM:2 * TM, :]
    gc = jnp.where(lane < N_HEAD, g_f, gtot - g_f + g)
    gc_ref[...] = gc
    bt_ref[...] = _sigmoid(ab)
    gt_ref[...] = gtot
    gct_ref[...] = gc.T
    ge_ref[...] = jnp.exp(gtot)
    out_refs = ((wf_ref, uf_ref, qf_ref, kf_ref, qkf_ref), (wb_ref, ub_ref, qb_ref, kb_ref, qkb_ref))
    sub8 = lax.broadcasted_iota(jnp.int32, (8, TM), 0)

    probs = [(h, d) for h in range(N_HEAD) for d in range(2)]
    kk, qk0 = {}, {}
    for h in range(N_HEAD):
        kb16 = s_ref[N_HEAD + h].astype(BF16)
        kk[h] = _dot_nt(kb16, kb16)
        qk0[h] = _dot_nt(s_ref[h].astype(BF16), kb16)
    tmat, dblk, lo_b, rhs_b = {}, {}, {}, {}
    for h, d in probs:
        q = s_ref[h]
        k = s_ref[N_HEAD + h]
        v = s_ref[2 * N_HEAD + h]
        idx = d * N_HEAD + h
        sel = lane == idx
        gc_col = jnp.sum(jnp.where(sel, gc_ref[...], 0.0), axis=1, keepdims=True)
        gt_col = jnp.sum(jnp.where(sel, gt_ref[...], 0.0), axis=1, keepdims=True)
        bt_col = jnp.sum(jnp.where(lane == idx + 2 * N_HEAD, bt_ref[...], 0.0), axis=1, keepdims=True)
        gc_row = jnp.sum(jnp.where(sub8 == idx, gct_ref[0:8, :], 0.0), axis=0, keepdims=True)
        e = jnp.exp(jnp.minimum(gc_col - gc_row, 0.0))
        a = kk[h] * (e * mk_ref[2 + d] * bt_col)
        eg = jnp.exp(gc_col)
        _, _, q_r, k_r, qk_r = out_refs[d]
        q_r[h] = (q * eg).astype(BF16)
        k_r[h] = (k * jnp.exp(gt_col - gc_col)).astype(BF16)
        qk_r[h] = (qk0[h] * (e * mk_ref[d])).astype(BF16)
        dm = a * mk_ref[4]
        tmat[h, d] = mk_ref[5] - dm
        lo_b[h, d] = (a - dm).astype(BF16)
        rhs_b[h, d] = jnp.concatenate([k * (bt_col * eg), v * bt_col], axis=1).astype(BF16)
        dblk[h, d] = dm.astype(BF16)
    pw = {pk: _dot(dblk[pk], dblk[pk]) for pk in probs}
    for m in range(3):
        for pk in probs:
            pb = pw[pk].astype(BF16)
            tmat[pk] = tmat[pk] + _dot(tmat[pk].astype(BF16), pb)
            if m < 2:
                pw[pk] = _dot(pb, pb)
    x0, mt, m2 = {}, {}, {}
    for pk in probs:
        tb = tmat[pk].astype(BF16)
        x0[pk] = _dot(tb, rhs_b[pk])
        mt[pk] = _dot(tb, lo_b[pk]).astype(BF16)
    for pk in probs:
        m2[pk] = _dot(mt[pk], mt[pk]).astype(BF16)
    ys = {pk: x0[pk] + _dot(m2[pk], x0[pk].astype(BF16)) for pk in probs}
    for h, d in probs:
        x = ys[h, d] - _dot(mt[h, d], ys[h, d].astype(BF16))
        w_r, u_r = out_refs[d][0], out_refs[d][1]
        w_r[h] = x[:, :HEAD_W].astype(BF16)
        u_r[h] = x[:, HEAD_W:]


def _dn_chunk_call(qkv, ab, conv_w, a_log, dt_bias):
    B, T, W = qkv.shape
    nt = T // TM
    hb = TM // 16
    last16 = T // 16 - 1
    al = jnp.zeros((1, 128), F32).at[0, :2 * N_HEAD].set(a_log.reshape(-1))
    dtb = jnp.zeros((1, 128), F32).at[0, :2 * N_HEAD].set(dt_bias.reshape(-1))
    shifts, sums, masks = _dn_constants()
    hspec = lambda w: pl.BlockSpec((None, N_HEAD, TM, w), lambda b, i: (b, 0, i, 0))
    hshape = lambda w, dt: jax.ShapeDtypeStruct((B, N_HEAD, T, w), dt)
    const = lambda a: pl.BlockSpec(a.shape, lambda b, i: (0,) * a.ndim)
    per_dir_specs = [hspec(HEAD_W), hspec(HEAD_W), hspec(HEAD_W), hspec(HEAD_W), hspec(TM)]
    per_dir_shapes = [hshape(HEAD_W, BF16), hshape(HEAD_W, F32), hshape(HEAD_W, BF16), hshape(HEAD_W, BF16),
                      hshape(TM, BF16)]
    return pl.pallas_call(
        functools.partial(_dn_chunk_kernel, nt=nt),
        grid=(B, nt),
        in_specs=[
            pl.BlockSpec((None, TM, W), lambda b, i: (b, i, 0)),
            pl.BlockSpec((None, 16, W), lambda b, i: (b, jnp.maximum(i * hb - 1, 0), 0)),
            pl.BlockSpec((None, 16, W), lambda b, i: (b, jnp.minimum((i + 1) * hb, last16), 0)),
            pl.BlockSpec((None, TM, 128), lambda b, i: (b, i, 0)),
            pl.BlockSpec((4, W), lambda b, i: (0, 0)),
            pl.BlockSpec((1, 128), lambda b, i: (0, 0)),
            pl.BlockSpec((1, 128), lambda b, i: (0, 0)),
            const(shifts), const(sums), const(masks),
        ],
        out_specs=per_dir_specs + per_dir_specs + [pl.BlockSpec((None, TM, 128), lambda b, i: (b, i, 0))],
        out_shape=per_dir_shapes + per_dir_shapes + [jax.ShapeDtypeStruct((B, T, 128), F32)],
        scratch_shapes=[
            pltpu.VMEM((3 * N_HEAD, TM, HEAD_W), F32),
            pltpu.VMEM((TM, 128), F32), pltpu.VMEM((TM, 128), F32),
            pltpu.VMEM((128, TM), F32), pltpu.VMEM((TM, 128), F32),
        ],
        compiler_params=_cparams(("arbitrary", "arbitrary")),
        name="dn_chunk",
    )(qkv, qkv, qkv, ab, conv_w, al, dtb, shifts, sums, masks)


def _dn_seq_kernel(wf_ref, uf_ref, qf_ref, kf_ref, qkf_ref, gef_ref, wb_ref, ub_ref, qb_ref, kb_ref, qkb_ref,
                   geb_ref, of_ref, ob_ref, st_ref):
    C = DN_CHUNK
    nc = TM // C

    @pl.when(pl.program_id(1) == 0)
    def _():
        st_ref[...] = jnp.zeros_like(st_ref)

    ins = ((wf_ref, uf_ref, qf_ref, kf_ref, qkf_ref, gef_ref, of_ref),
           (wb_ref, ub_ref, qb_ref, kb_ref, qkb_ref, geb_ref, ob_ref))
    chains = [(d, bb, h) for d in range(2) for bb in range(DN_SEQ_BB) for h in range(N_HEAD)]
    for ci in range(nc):
        st, sb, ws, qs, vb = {}, {}, {}, {}, {}
        for ch in chains:
            d, bb, h = ch
            c = ci if d == 0 else nc - 1 - ci
            rows = slice(c * C, (c + 1) * C)
            st[ch] = st_ref[(d * DN_SEQ_BB + bb) * N_HEAD + h]
            sb[ch] = st[ch].astype(BF16)
            ws[ch] = _dot(ins[d][0][bb, h, rows, :], sb[ch])
            qs[ch] = _dot(ins[d][2][bb, h, rows, :], sb[ch])
        for ch in chains:
            d, bb, h = ch
            c = ci if d == 0 else nc - 1 - ci
            rows = slice(c * C, (c + 1) * C)
            vb[ch] = (ins[d][1][bb, h, rows, :] - ws[ch]).astype(BF16)
        for ch in chains:
            d, bb, h = ch
            c = ci if d == 0 else nc - 1 - ci
            rows = slice(c * C, (c + 1) * C)
            _, _, _, k_r, qk_r, ge_r, o_r = ins[d]
            si = d * N_HEAD + h
            ge = ge_r[bb, c * C:c * C + 8, :][0:1, si:si + 1]
            o_r[bb, rows, h * HEAD_W:(h + 1) * HEAD_W] = qs[ch] + _dot(qk_r[bb, h, rows, c * C:(c + 1) * C], vb[ch])
            st_ref[(d * DN_SEQ_BB + bb) * N_HEAD + h] = ge * st[ch] + _dot_tn(k_r[bb, h, rows, :], vb[ch])


def _dn_seq_call(chunk_outs):
    wf, uf, qf, kf, qkf, wb, ub, qb, kb, qkb, ge = chunk_outs
    B, _, T, _ = wf.shape
    nt = T // TM
    bb = DN_SEQ_BB
    tf = lambda s: s
    tb = lambda s: jnp.where(s == 0, 0, nt - s)
    hspec = lambda w, t: pl.BlockSpec((bb, N_HEAD, TM, w), lambda b, s: (b, 0, t(s), 0))
    gspec = lambda t: pl.BlockSpec((bb, TM, 128), lambda b, s: (b, t(s), 0))
    dir_specs = lambda t: [hspec(HEAD_W, t), hspec(HEAD_W, t), hspec(HEAD_W, t), hspec(HEAD_W, t), hspec(TM, t),
                           gspec(t)]
    ospec = lambda t: pl.BlockSpec((bb, TM, MIX_W), lambda b, s: (b, t(s), 0))
    return pl.pallas_call(
        _dn_seq_kernel,
        grid=(B // bb, nt),
        in_specs=dir_specs(tf) + dir_specs(tb),
        out_specs=[ospec(tf), ospec(tb)],
        out_shape=[jax.ShapeDtypeStruct((B, T, MIX_W), F32)] * 2,
        scratch_shapes=[pltpu.VMEM((2 * bb * N_HEAD, HEAD_W, HEAD_W), F32)],
        compiler_params=_cparams(("arbitrary", "arbitrary")),
        name="dn_seq",
    )(wf, uf, qf, kf, qkf, ge, wb, ub, qb, kb, qkb, ge)


def _gelu_tanh(x):
    return 0.5 * x * (1.0 + jnp.tanh(math.sqrt(2.0 / math.pi) * (x + 0.044715 * (x * x * x))))


def _lru_kernel(x_ref, y_ref, cw_ref, cb_ref, wg_ref, bg_ref, lam_ref, sh_ref, o_ref, hf_ref, xc_ref, *, nt):
    CW = x_ref.shape[1]
    T = x_ref.shape[0]
    G = TM // 8
    row8 = lax.broadcasted_iota(jnp.int32, (8, CW), 0)
    sub = lax.broadcasted_iota(jnp.int32, (G, 8, CW), 1)

    def conv(t):
        base = pl.multiple_of(t * TM, TM)
        xb = x_ref[pl.ds(base, TM), :]
        xp = x_ref[pl.ds(pl.multiple_of(jnp.maximum(base - 16, 0), 16), 16), :].astype(F32)
        xn = x_ref[pl.ds(pl.multiple_of(jnp.minimum(base + TM, T - 16), 16), 16), :].astype(F32)
        vp, vn = _halo_valid(t, nt)
        w = cw_ref[...]
        y = (w[0:1, :] * _dot(sh_ref[0], xb) + w[1:2, :] * xb.astype(F32)
             + w[2:3, :] * _dot(sh_ref[1], xb) + w[3:4, :] * _dot(sh_ref[2], xb))
        n0, n1 = xn[0:1, :] * vn, xn[1:2, :] * vn
        top = jnp.where(row8 == 0, w[0:1, :] * (xp[15:16, :] * vp), 0.0)
        bot = jnp.where(row8 == 6, w[3:4, :] * n0, jnp.where(row8 == 7, w[2:3, :] * n0 + w[3:4, :] * n1, 0.0))
        return jnp.concatenate([y[0:8, :] + top, y[8:TM - 8, :], y[TM - 8:TM, :] + bot], axis=0) + cb_ref[...]

    def coeffs(xc, d):
        z = _dot(xc.astype(BF16), wg_ref[:, d * 2 * CW:(d + 1) * 2 * CW]) + bg_ref[:, d * 2 * CW:(d + 1) * 2 * CW]
        rg = _sigmoid(z[:, :CW])
        ig = _sigmoid(z[:, CW:])
        log_a = (-LRU_C * _softplus(-lam_ref[d:d + 1, :])) * rg
        a = jnp.exp(log_a)
        om = 1.0 - a * a
        b = jnp.where(om > 0.0, om * lax.rsqrt(om), 0.0) * (ig * xc)
        return a, b

    def scan_tile(a, b, h_in, d):
        a = a.reshape(G, 8, CW)
        b = b.reshape(G, 8, CW)
        for s in (1, 2, 4):
            if d == 0:
                a_s, b_s, valid = pltpu.roll(a, s, 1), pltpu.roll(b, s, 1), sub >= s
            else:
                a_s, b_s, valid = pltpu.roll(a, 8 - s, 1), pltpu.roll(b, 8 - s, 1), sub < 8 - s
            b = jnp.where(valid, a * b_s + b, b)
            a = jnp.where(valid, a * a_s, a)
        hs = [None] * G
        h = h_in
        for g in (range(G) if d == 0 else range(G - 1, -1, -1)):
            hg = b[g] + a[g] * h
            hs[g] = hg
            h = hg[7:8, :] if d == 0 else hg[0:1, :]
        return jnp.concatenate(hs, axis=0), h

    def fwd(t, h):
        rows = pl.ds(pl.multiple_of(t * TM, TM), TM)
        xc = conv(t)
        xc_ref[rows, :] = xc
        a, b = coeffs(xc, 0)
        hf, h = scan_tile(a, b, h, 0)
        hf_ref[rows, :] = hf
        return h

    lax.fori_loop(0, nt, fwd, jnp.zeros((1, CW), F32))

    def bwd(s, h):
        t = jnp.where(s == 0, 0, nt - s)
        rows = pl.ds(pl.multiple_of(t * TM, TM), TM)
        a, b = coeffs(xc_ref[rows, :], 1)
        hb, h = scan_tile(a, b, h, 1)
        y = y_ref[rows, :].astype(F32)
        o_ref[rows, :] = ((hf_ref[rows, :] + hb) * _gelu_tanh(y)).astype(o_ref.dtype)
        return h

    lax.fori_loop(0, nt, bwd, jnp.zeros((1, CW), F32))


def _lru_call(x, y, conv_w, conv_b, wg, bg, lam):
    B, T, W = x.shape
    cw = W // 2
    nt = T // TM
    shifts = _dn_constants()[0]
    return pl.pallas_call(
        functools.partial(_lru_kernel, nt=nt),
        grid=(B, 2),
        in_specs=[
            pl.BlockSpec((None, T, cw), lambda b, p: (b, 0, p)),
            pl.BlockSpec((None, T, cw), lambda b, p: (b, 0, p)),
            pl.BlockSpec((4, cw), lambda b, p: (0, p)),
            pl.BlockSpec((1, cw), lambda b, p: (0, p)),
            pl.BlockSpec((None, cw, 4 * cw), lambda b, p: (p, 0, 0)),
            pl.BlockSpec((None, 1, 4 * cw), lambda b, p: (p, 0, 0)),
            pl.BlockSpec((2, cw), lambda b, p: (0, p)),
            pl.BlockSpec(shifts.shape, lambda b, p: (0, 0, 0)),
        ],
        out_specs=pl.BlockSpec((None, T, cw), lambda b, p: (b, 0, p)),
        out_shape=jax.ShapeDtypeStruct((B, T, W), BF16),
        scratch_shapes=[pltpu.VMEM((T, cw), F32), pltpu.VMEM((T, cw), F32)],
        compiler_params=_cparams(("arbitrary", "arbitrary")),
        name="rglru",
    )(x, y, conv_w, conv_b, wg, bg, lam, shifts)


def _rope(x, cs, sn, first_half):
    swapped = jnp.where(first_half, pltpu.roll(x, 112, 1), pltpu.roll(x, 16, 1))
    return x * cs + swapped * sn


def _attn_kernel(*refs, lam_init, n_q, nt):
    q_refs = refs[:n_q]
    k_ref, v_ref = refs[n_q:n_q + 2]
    tab_refs = refs[n_q + 2:3 * n_q + 2]
    ck_ref, sk_ref, lv_ref, nw_ref, o_ref, kr_ref, vt_ref, qt_ref, st0_ref, st1_ref, acc_ref = refs[3 * n_q + 2:]
    lane = lax.broadcasted_iota(jnp.int32, (TM, HEAD_W), 1)
    first_half = (lane % 32) < 16
    heads = [(hh, slice(hh * HEAD_W, (hh + 1) * HEAD_W)) for hh in range(ATTN_HP)]
    chains = [(qi, hh) for qi in range(n_q) for hh in range(ATTN_HP)]

    @pl.when(pl.program_id(2) == 0)
    def _():
        extra = (lax.broadcasted_iota(jnp.int32, (VT_ROWS - HEAD_W, TM), 0) == 0).astype(BF16)

        def prep_kv(t, carry):
            rows = pl.ds(pl.multiple_of(t * TM, TM), TM)
            for hh, cols in heads:
                kr_ref[hh, rows, :] = _rope(k_ref[rows, cols].astype(F32), ck_ref[rows, :], sk_ref[rows, :],
                                            first_half).astype(BF16)
                vt_ref[hh, 0:HEAD_W, rows] = v_ref[rows, cols].astype(F32).T.astype(BF16)
                vt_ref[hh, HEAD_W:VT_ROWS, rows] = extra
            return carry
        lax.fori_loop(0, nt, prep_kv, 0)

    for qi, hh in chains:
        q = _rope(q_refs[qi][:, heads[hh][1]].astype(F32), tab_refs[2 * qi][...], tab_refs[2 * qi + 1][...],
                  first_half) * (64 ** -0.5 * math.log2(math.e))
        qt_ref[qi * ATTN_HP + hh] = jnp.concatenate(
            [jnp.where(lane < 64, q, 0.0).T, jnp.where(lane >= 64, q, 0.0).T], axis=1).astype(BF16)
        acc_ref[qi * ATTN_HP + hh] = jnp.zeros((VT_ROWS, 2 * TM), F32)

    blocks = [slice(0, TM)] + [slice(s0 * TM, min(s0 + ATTN_KB, nt) * TM) for s0 in range(1, nt, ATTN_KB)]

    def scores(keys, st_ref):
        nk = keys.stop - keys.start
        mx = {}
        for qi, hh in chains:
            ci = qi * ATTN_HP + hh
            st = _dot(kr_ref[hh, keys, :], qt_ref[ci])
            st_ref[ci, 0:nk, :] = st
            mx[ci] = jnp.max(st, axis=0, keepdims=True)
        return mx

    def softmax_pv(keys, st_ref, ms, mx):
        nk = keys.stop - keys.start
        out = {}
        for qi, hh in chains:
            ci = qi * ATTN_HP + hh
            m_new = jnp.maximum(ms[ci], mx[ci])
            alpha = jnp.exp2(ms[ci] - m_new)
            for half in range(2):
                hc = slice(half * TM, (half + 1) * TM)
                p = jnp.exp2(st_ref[ci, 0:nk, hc] - m_new[:, hc]).astype(BF16)
                acc_ref[ci, :, hc] = alpha[:, hc] * acc_ref[ci, :, hc] + _dot(vt_ref[hh, :, keys], p)
            out[ci] = m_new
        return out

    bufs = (st0_ref, st1_ref)
    mx = scores(blocks[0], st0_ref)
    ms = {qi * ATTN_HP + hh: jnp.full((1, 2 * TM), -1e30, F32) for qi, hh in chains}
    for j, keys in enumerate(blocks):
        mx_next = scores(blocks[j + 1], bufs[(j + 1) % 2]) if j + 1 < len(blocks) else None
        ms = softmax_pv(keys, bufs[j % 2], ms, mx)
        mx = mx_next
    lv = lv_ref[...]
    lam = (jnp.exp(jnp.sum(lv[0:1, :] * lv[1:2, :], axis=-1, keepdims=True))
           - jnp.exp(jnp.sum(lv[2:3, :] * lv[3:4, :], axis=-1, keepdims=True)) + lam_init)
    for qi, hh in chains:
        ci = qi * ATTN_HP + hh
        on = acc_ref[ci, 0:HEAD_W, :] * (1.0 / acc_ref[ci, HEAD_W:HEAD_W + 1, :])
        o = (on[:, :TM] - lam * on[:, TM:]).T
        ms_o = jnp.mean(o * o, axis=-1, keepdims=True)
        o_ref[qi * TM:(qi + 1) * TM, heads[hh][1]] = (
            o * lax.rsqrt(ms_o + 1e-5) * nw_ref[...] * (1.0 - lam_init)).astype(o_ref.dtype)


def _attn_call(qkv, cos_t, sin_t, lam_vecs, sub_norm, lam_init, ctx):
    B, T, _ = qkv.shape
    pw = ATTN_HP * HEAD_W
    npair = N_HEAD // ATTN_HP
    if ctx:
        n_q, nk, nsteps, q_tiles = 1, 1, 1, [lambda s: 0]
    else:
        assert (T // TM - 1) % 2 == 0, "latent query tiles are processed in pairs"
        n_q, nk, nsteps = 2, T // TM, (T // TM - 1) // 2
        q_tiles = [lambda s: 2 * s + 1, lambda s: 2 * s + 2]
    nch = n_q * ATTN_HP
    qspec = lambda t: pl.BlockSpec((None, TM, pw), lambda b, h, s: (b, t(s), h))
    tspec = lambda t: pl.BlockSpec((TM, HEAD_W), lambda b, h, s: (t(s), 0))
    tabs = [tspec(t) for t in q_tiles for _ in range(2)]
    kern = functools.partial(_attn_kernel, lam_init=lam_init, n_q=n_q, nt=nk)
    return pl.pallas_call(
        kern,
        grid=(B, npair, nsteps),
        in_specs=[qspec(t) for t in q_tiles] + [
            pl.BlockSpec((None, nk * TM, pw), lambda b, h, s: (b, 0, npair + h)),
            pl.BlockSpec((None, nk * TM, pw), lambda b, h, s: (b, 0, 2 * npair + h)),
        ] + tabs + [
            pl.BlockSpec((nk * TM, HEAD_W), lambda b, h, s: (0, 0)),
            pl.BlockSpec((nk * TM, HEAD_W), lambda b, h, s: (0, 0)),
            pl.BlockSpec((4, 64), lambda b, h, s: (0, 0)),
            pl.BlockSpec((1, HEAD_W), lambda b, h, s: (0, 0)),
        ],
        out_specs=pl.BlockSpec((None, n_q * TM, pw), lambda b, h, s: (b, s, h)),
        out_shape=jax.ShapeDtypeStruct((B, nsteps * n_q * TM, MIX_W), BF16),
        scratch_shapes=[pltpu.VMEM((ATTN_HP, nk * TM, HEAD_W), BF16), pltpu.VMEM((ATTN_HP, VT_ROWS, nk * TM), BF16),
                        pltpu.VMEM((nch, HEAD_W, 2 * TM), BF16),
                        pltpu.VMEM((nch, ATTN_KB * TM, 2 * TM), F32), pltpu.VMEM((nch, ATTN_KB * TM, 2 * TM), F32),
                        pltpu.VMEM((nch, VT_ROWS, 2 * TM), F32)],
        compiler_params=_cparams(("arbitrary", "arbitrary", "arbitrary")),
        name="diff_attn_ctx" if ctx else "diff_attn",
    )(*([qkv] * (n_q + 2)), *([cos_t, sin_t] * n_q), cos_t, sin_t, lam_vecs, sub_norm)


def _mix_ffn_kernel(*refs, n_h, t_off, final):
    mod_ref, dnf_ref, dnb_ref, z_ref, yb_ref = refs[n_h:n_h + 5]
    rest = refs[n_h + 5:]
    (ycl_ref, g_ref, dnw_ref, wb_ref, wo_ref, nw_ref, wg_ref, wu_ref, wd_ref, fw_ref, o_ref) = rest[-11:]
    yc = ycl_ref[...]
    if len(rest) == 12:
        yc = jnp.where(pl.program_id(1) + t_off == 0, rest[0][...], yc)
    dnw = dnw_ref[...]
    ya_parts = []
    for hh in range(N_HEAD):
        cols = slice(hh * HEAD_W, (hh + 1) * HEAD_W)
        o = dnf_ref[:, cols] + dnb_ref[:, cols]
        ms = jnp.mean(o * o, axis=-1, keepdims=True)
        z = z_ref[:, cols].astype(F32)
        ya_parts.append(((o * lax.rsqrt(ms + 1e-6) * dnw) * (z * _sigmoid(z))).astype(BF16))
    ys = (jnp.concatenate(ya_parts, axis=1), yb_ref[...], yc)
    mix = None
    for n in range(3):
        up = _dot(ys[n], wb_ref[n])
        term = _sigmoid(g_ref[:, n * D_MODEL:(n + 1) * D_MODEL].astype(F32)) * up
        mix = term if mix is None else mix + term
    h = _read_stream(refs[:n_h]) + mod_ref[2:3, :] * _dot(mix.astype(BF16), wo_ref[...])

    x = _norm_modulate(h, nw_ref[...], mod_ref[3:4, :], mod_ref[4:5, :]).astype(BF16)
    acc = None
    for c in range(0, D_FF, FF_CHUNK):
        ce = min(c + FF_CHUNK, D_FF)
        g = _dot(x, wg_ref[:, c:ce])
        u = _dot(x, wu_ref[:, c:ce])
        part = _dot(((g * _sigmoid(g)) * u).astype(BF16), wd_ref[c:ce, :])
        acc = part if acc is None else acc + part
    out = h + mod_ref[5:6, :] * acc
    if final:
        ms = jnp.mean(out * out, axis=-1, keepdims=True)
        out = out * lax.rsqrt(ms + 1e-6) * fw_ref[...]
    o_ref[...] = out


def _mix_ffn_call(hs, mod, dnf, dnb, z, yb, yc_ctx, yc_lat, gates, dn_norm, wb, wo, nw, wg, wu, wd, fw, t_off, final):
    B = hs[0].shape[0]
    nt = sum(h.shape[1] for h in hs) // TM - t_off
    row = lambda w: pl.BlockSpec((None, TM, w), lambda b, i: (b, i + t_off, 0))
    const = lambda a: pl.BlockSpec(a.shape, lambda b, i: (0,) * a.ndim, pipeline_mode=pl.Buffered(1))
    yc_specs = [pl.BlockSpec((None, TM, MIX_W), lambda b, i: (b, jnp.maximum(i + t_off - 1, 0), 0))]
    yc_args = [yc_lat]
    if yc_ctx is not None:
        yc_specs.insert(0, pl.BlockSpec((None, TM, MIX_W), lambda b, i: (b, 0, 0)))
        yc_args.insert(0, yc_ctx)
    return pl.pallas_call(
        functools.partial(_mix_ffn_kernel, n_h=len(hs), t_off=t_off, final=final),
        grid=(B, nt),
        in_specs=_stream_specs(hs, t_off) + [
            pl.BlockSpec((None, None, 6, D_MODEL), lambda b, i: _mod_index(b, i + t_off)),
            row(MIX_W), row(MIX_W), row(MIX_W), row(MIX_W), *yc_specs, row(3 * D_MODEL),
            const(dn_norm), const(wb), const(wo), const(nw), const(wg), const(wu), const(wd), const(fw),
        ],
        out_specs=pl.BlockSpec((None, TM, D_MODEL), lambda b, i: (b, i, 0)),
        out_shape=jax.ShapeDtypeStruct((B, nt * TM, D_MODEL), F32),
        compiler_params=_cparams(("arbitrary", "arbitrary")),
        name="mix_ffn",
    )(*hs, mod, dnf, dnb, z, yb, *yc_args, gates, dn_norm, wb, wo, nw, wg, wu, wd, fw)


def _reorder_w_in(w):
    parts = [w[:, 0:1536], w[:, 1536:2048], w[:, 2064:2576], w[:, 2576:3088], w[:, 3088:4624], w[:, 4624:7696],
             w[:, 2048:2064], jnp.zeros((w.shape[0], IN_COLS_PAD - 7696), w.dtype)]
    return jnp.concatenate(parts, axis=1).astype(BF16)


def _block_diag(w):
    g, n, _ = w.shape
    return jnp.einsum('gij,gh->gihj', w, jnp.eye(g, dtype=w.dtype)).reshape(g * n, g * n)


def _lru_gate_weights(wa, ba, wi, bi):
    half = LRU_G // 2
    ws, bs = [], []
    for p in range(2):
        grp = slice(p * half, (p + 1) * half)
        ch = slice(p * MIX_W // 2, (p + 1) * MIX_W // 2)
        cols, bcols = [], []
        for d in range(2):
            cols += [_block_diag(wa[d, grp]), _block_diag(wi[d, grp])]
            bcols += [ba[d, ch], bi[d, ch]]
        ws.append(jnp.concatenate(cols, axis=1))
        bs.append(jnp.concatenate(bcols)[None, :])
    return jnp.stack(ws).astype(BF16), jnp.stack(bs)


def _rope_tables(seq):
    rows = seq // GRID_W
    rowp = jnp.repeat(jnp.arange(rows, dtype=F32), GRID_W)
    colp = jnp.tile(jnp.arange(GRID_W, dtype=F32), rows)
    inv = ROPE_BASE ** (-jnp.arange(ROPE_NF, dtype=F32) / ROPE_NF)
    ar, ac = rowp[:, None] * inv, colp[:, None] * inv
    cs = jnp.concatenate([jnp.cos(ar), jnp.cos(ar), jnp.cos(ac), jnp.cos(ac)], axis=1)
    sn = jnp.concatenate([-jnp.sin(ar), jnp.sin(ar), -jnp.sin(ac), jnp.sin(ac)], axis=1)
    cs = jnp.concatenate([jnp.ones((CTX_LEN, 64), F32), cs], axis=0)
    sn = jnp.concatenate([jnp.zeros((CTX_LEN, 64), F32), sn], axis=0)
    return jnp.tile(cs, (1, 2)), jnp.tile(sn, (1, 2))


def kernel(x, c, ctx, c_ctx, w_mod, b_mod, norm_mix, norm_ffn, w_in, dn_conv, dn_a_log, dn_dt_bias, dn_norm, lru_conv_w, lru_conv_b, lru_wa, lru_ba, lru_wi, lru_bi, lru_lambda, da_lambda, da_norm, w_branch, w_out, w_ffn_gate, w_ffn_up, w_ffn_down, norm_final):
    B, S, D = x.shape
    depth = w_mod.shape[0]
    rows = jnp.concatenate([c, c_ctx[None, :], jnp.zeros((16 - B - 1, D), F32)], axis=0)
    mod = _mod_call(rows, w_mod, b_mod).reshape(depth, 16, 6, D)
    cos_t, sin_t = _rope_tables(S)
    hs = (ctx, x)
    for l in range(depth):
        last = l == depth - 1
        t_off = 1 if last else 0
        lam_init = 0.8 - 0.6 * math.exp(-0.3 * l)
        mod_l = jnp.stack([jnp.broadcast_to(mod[l, B][None], (B, 6, D)), mod[l, :B]], axis=1)
        dn_qkv, dn_z, lru_x, lru_y, da_qkv, gates, ab = _inproj_call(
            hs, mod_l, norm_mix[l][None, :], _reorder_w_in(w_in[l]))
        dn_f, dn_b = _dn_seq_call(_dn_chunk_call(dn_qkv, ab, dn_conv[l], dn_a_log[l], dn_dt_bias[l]))
        wg, bg = _lru_gate_weights(lru_wa[l], lru_ba[l], lru_wi[l], lru_bi[l])
        yb = _lru_call(lru_x, lru_y, lru_conv_w[l], lru_conv_b[l][None, :], wg, bg, lru_lambda[l])
        attn = functools.partial(_attn_call, da_qkv, cos_t, sin_t, da_lambda[l], da_norm[l][None, :], lam_init)
        yc_ctx, yc_lat = (None if last else attn(True)), attn(False)
        hs = (_mix_ffn_call(hs, mod_l, dn_f, dn_b, dn_z, yb, yc_ctx, yc_lat, gates, dn_norm[l][None, :],
                            w_branch[l].astype(BF16), w_out[l].astype(BF16), norm_ffn[l][None, :],
                            w_ffn_gate[l].astype(BF16), w_ffn_up[l].astype(BF16), w_ffn_down[l].astype(BF16),
                            norm_final[None, :], t_off, last),)
    return hs[0]
```
